```python
import jax, jax.numpy as jnp
from jax import lax
import numpy as np

D_MODEL = 1024
BATCH = 16
SEQ = 4096
DEPTH = 2
DEC_BATCH = 16
DEC_SEQ = 16
PAST_LEN = 1024

CHUNK = 64
MIX_WIDTH = D_MODEL
POOL_WINDOWS = (2, 4, 8, 16)
POOL_GROUP_DIM = D_MODEL // 16
POOL_WIDTH = len(POOL_WINDOWS) * POOL_GROUP_DIM
POOL_HIST = max(POOL_WINDOWS) - 1
CONV_WIDTH = 3 * D_MODEL // 8
CONV_K = 3
CONV_HIST = CONV_K - 1
GMLP_WIDTH = 3 * D_MODEL // 8
GMLP_HEADS = 6
GMLP_HEAD_DIM = GMLP_WIDTH // GMLP_HEADS
GMLP_CHUNK = 128
SPLIT_SIZES = (POOL_WIDTH, POOL_WIDTH,
               CONV_WIDTH, CONV_WIDTH, CONV_WIDTH, CONV_WIDTH,
               GMLP_WIDTH, GMLP_WIDTH, GMLP_WIDTH)
PROJ_WIDTH = sum(SPLIT_SIZES)
EPS = 1e-6

kernel_name = "hybrid_pool_conv_gmlp_stream_step"


def rmsnorm(x, g):
    x32 = x.astype(jnp.float32)
    y = x32 * lax.rsqrt(jnp.mean(x32 * x32, axis=-1, keepdims=True) + EPS)
    return (y * g.astype(jnp.float32)).astype(x.dtype)


def layernorm(x, g, b):
    x32 = x.astype(jnp.float32)
    mu = jnp.mean(x32, axis=-1, keepdims=True)
    var = jnp.mean(jnp.square(x32 - mu), axis=-1, keepdims=True)
    y = (x32 - mu) * lax.rsqrt(var + EPS)
    return (y * g.astype(jnp.float32) + b.astype(jnp.float32)).astype(x.dtype)


def pool_mix(p, hist, pos0, w_pool, pool_scale):
    L = p.shape[1]
    full = jnp.concatenate([hist, p], axis=1)
    cs = jnp.cumsum(full.astype(jnp.float32), axis=1)
    cs = jnp.pad(cs, ((0, 0), (1, 0), (0, 0)))
    pos = pos0 + jnp.arange(L, dtype=jnp.int32)
    outs = []
    for g, w in enumerate(POOL_WINDOWS):
        sl = slice(g * POOL_GROUP_DIM, (g + 1) * POOL_GROUP_DIM)
        wsum = cs[:, POOL_HIST + 1:, sl] - cs[:, POOL_HIST + 1 - w:POOL_HIST + 1 - w + L, sl]
        cnt = jnp.minimum(pos + 1, w).astype(jnp.float32)
        d = (wsum / cnt[None, :, None]).astype(p.dtype) - p[..., sl]
        outs.append(jnp.einsum('bld,de->ble', d, w_pool[g]))
    out = jnp.concatenate(outs, axis=-1) * pool_scale
    return out, full[:, -POOL_HIST:]


def conv_mix(q, hist, w_conv):
    L = q.shape[1]
    full = jnp.concatenate([hist, q], axis=1)
    out = sum(w_conv[k] * full[:, k:k + L] for k in range(CONV_K))
    return out, full[:, -CONV_HIST:]


def gmlp_mix(u, v, w_s, b_s):
    B, L, _ = v.shape
    n = min(L, GMLP_CHUNK)
    blk = jnp.arange(n) // CHUNK
    mask = blk[None, :] <= blk[:, None]
    w = jnp.where(mask[None], w_s[:, :n, :n], 0)
    vh = v.reshape(B, L // n, n, GMLP_HEADS, GMLP_HEAD_DIM)
    mixed = jnp.einsum('hts,bcshd->bcthd', w, vh) + b_s[:, :n].T[None, None, :, :, None]
    return u * mixed.reshape(B, L, GMLP_WIDTH)


def layer(x, c, pool_hist, conv_hist, pos0, norm_g, w_ada, b_ada, w_in, w_pool, pool_scale,
          w_conv, v_norm_g, v_norm_b, w_s, b_s, w_out):
    mod = jax.nn.silu(c) @ w_ada + b_ada
    shift, scale, gate = jnp.split(mod, 3, axis=-1)
    h = rmsnorm(x, norm_g) * (1 + scale[:, None]) + shift[:, None]
    z = h @ w_in
    p, z_pool, gb, gc, hx, z_conv, u, v, z_gmlp = jnp.split(
        z, list(np.cumsum(SPLIT_SIZES)[:-1]), axis=-1)
    y_pool, new_pool = pool_mix(p, pool_hist, pos0, w_pool, pool_scale)
    y_conv, new_conv = conv_mix(gc * hx, conv_hist, w_conv)
    vn = layernorm(v, v_norm_g, v_norm_b)
    y_gmlp = gmlp_mix(u, vn, w_s, b_s)
    y = jnp.concatenate([y_pool * jax.nn.silu(z_pool),
                         gb * y_conv * jax.nn.silu(z_conv),
                         y_gmlp * jax.nn.silu(z_gmlp)], axis=-1) @ w_out
    return x + gate[:, None] * y, new_pool, new_conv, vn


def setup_inputs(seed: int = 0) -> dict:
    key = jax.random.key(seed)
    ks = jax.random.split(key, 20)
    nrm = lambda k, s, sc: jax.random.normal(k, s, jnp.float32) * sc
    D = D_MODEL
    return {
        "x_prompt": nrm(ks[0], (BATCH, SEQ, D), 1.0),
        "x_sample": nrm(ks[1], (DEC_BATCH, DEC_SEQ, D), 1.0),
        "state_pool": nrm(ks[2], (DEPTH, DEC_BATCH, POOL_HIST, POOL_WIDTH), 1.0),
        "state_conv": nrm(ks[3], (DEPTH, DEC_BATCH, CONV_HIST, CONV_WIDTH), 1.0),
        "c_prompt": nrm(ks[4], (BATCH, D), 1.0),
        "c_sample": nrm(ks[5], (DEC_BATCH, D), 1.0),
        "norm_g": 1.0 + nrm(ks[6], (DEPTH, D), 0.05),
        "w_ada": nrm(ks[7], (DEPTH, D, 3 * D), 0.5 * D ** -0.5),
        "b_ada": nrm(ks[8], (DEPTH, 3 * D), 0.02),
        "w_in": nrm(ks[9], (DEPTH, D, PROJ_WIDTH), D ** -0.5),
        "w_pool": nrm(ks[10], (DEPTH, len(POOL_WINDOWS), POOL_GROUP_DIM, POOL_GROUP_DIM), POOL_GROUP_DIM ** -0.5),
        "pool_scale": 1.0 + nrm(ks[11], (DEPTH, POOL_WIDTH), 0.1),
        "w_conv": nrm(ks[12], (DEPTH, CONV_K, CONV_WIDTH), CONV_K ** -0.5),
        "v_norm_g": 1.0 + nrm(ks[13], (DEPTH, GMLP_WIDTH), 0.05),
        "v_norm_b": nrm(ks[14], (DEPTH, GMLP_WIDTH), 0.02),
        "w_s": nrm(ks[15], (DEPTH, GMLP_HEADS, GMLP_CHUNK, GMLP_CHUNK), GMLP_CHUNK ** -0.5),
        "b_s": 1.0 + nrm(ks[16], (DEPTH, GMLP_HEADS, GMLP_CHUNK), 0.1),
        "w_out": nrm(ks[17], (DEPTH, MIX_WIDTH, D), MIX_WIDTH ** -0.5),
        "final_norm_g": 1.0 + nrm(ks[18], (D,), 0.05),
    }


def reference(x_prompt, x_sample, state_pool, state_conv, c_prompt, c_sample, norm_g, w_ada, b_ada,
              w_in, w_pool, pool_scale, w_conv, v_norm_g, v_norm_b, w_s, b_s, w_out, final_norm_g):
    xp, xs = x_prompt, x_sample
    zero_pool = jnp.zeros((xp.shape[0], POOL_HIST, POOL_WIDTH), xp.dtype)
    zero_conv = jnp.zeros((xp.shape[0], CONV_HIST, CONV_WIDTH), xp.dtype)
    pool_p, conv_p, pool_s, conv_s, v_s = [], [], [], [], []
    for l in range(DEPTH):
        params = (norm_g[l], w_ada[l], b_ada[l], w_in[l], w_pool[l], pool_scale[l], w_conv[l],
                  v_norm_g[l], v_norm_b[l], w_s[l], b_s[l], w_out[l])
        xp, npool, nconv, _ = layer(xp, c_prompt, zero_pool, zero_conv, 0, *params)
        pool_p.append(npool)
        conv_p.append(nconv)
        xs, npool, nconv, vrows = layer(xs, c_sample, state_pool[l], state_conv[l], PAST_LEN, *params)
        pool_s.append(npool)
        conv_s.append(nconv)
        v_s.append(vrows)
    y_prompt = rmsnorm(xp, final_norm_g)
    y_sample = rmsnorm(xs, final_norm_g)
    return (y_prompt, y_sample, jnp.stack(pool_p), jnp.stack(conv_p), jnp.stack(pool_s),
            jnp.stack(conv_s), jnp.stack(v_s))
```

```python
import functools

import numpy as np
import jax
import jax.numpy as jnp
from jax import lax
from jax.experimental import pallas as pl
from jax.experimental.pallas import tpu as pltpu

D_MODEL = 1024
POOL_WINDOWS = (2, 4, 8, 16)
POOL_GROUP = 64
POOL_W = 256
CONV_W = 384
CONV_K = 3
GMLP_W = 384
GMLP_HEADS = 6
GMLP_HEAD_DIM = 64
GMLP_CHUNK = 128
CHUNK = 64
PROJ_W = 3200
PAST_LEN = 1024
EPS = 1e-6

_SPLITS = (256, 256, 384, 384, 384, 384, 384, 384, 384)
_OFFS = tuple(int(v) for v in np.cumsum((0,) + _SPLITS))

POOL_HIST_ROWS = 16
CONV_HIST_ROWS = 8
LANES = 128
SEQ_TILE = 256
VMEM_LIMIT_BYTES = 56 * 1024 * 1024


def _silu(z):
    return z * (1.0 / (1.0 + jnp.exp(-z)))


def _mod_kernel(c_ref, w_ref, b_ref, o_ref):
    o_ref[0, 0] = jnp.dot(_silu(c_ref[...]), w_ref[0], preferred_element_type=jnp.float32) + b_ref[0]


def _ext_rows(hist3, val3):
    nseg, h, c = hist3.shape
    l = val3.shape[1]
    return jnp.concatenate([hist3, val3], axis=1).reshape(nseg * (h + l), c)


def _unext(e2, nseg, h, l):
    return e2.reshape(nseg, h + l, e2.shape[-1])[:, h:, :]


def _trunk_kernel(*refs, nseg, rows, gchunk, carried, pos0):
    (x_ref, mod_ref, norm_g_ref, w_in_ref, wpool_ref, pscale_ref, wconv_ref, vng_ref, vnb_ref,
     glhs_ref, gbias_ref, w_out_ref, fin_g_ref) = refs[:13]
    if carried:
        y_ref, pool_out_ref, conv_out_ref = refs[13:]
        pool_in_ref, conv_in_ref, vs_ref = pool_out_ref, conv_out_ref, None
        tile = pl.program_id(1)
        pos_base = tile * rows + pos0

        @pl.when(tile == 0)
        def _():
            pool_out_ref[...] = jnp.zeros_like(pool_out_ref)
            conv_out_ref[...] = jnp.zeros_like(conv_out_ref)
    else:
        pool_in_ref, conv_in_ref = refs[13:15]
        y_ref, pool_out_ref, conv_out_ref, vs_ref = refs[15:]
        pos_base = pos0

    L = rows
    R = nseg * L
    depth = w_in_ref.shape[0]
    lane = lax.broadcasted_iota(jnp.int32, (1, LANES), 1)
    low_half = lane < POOL_GROUP

    inv_w = jnp.concatenate(
        [jnp.full((1, POOL_GROUP), 1.0 / w, jnp.float32) for w in POOL_WINDOWS], axis=-1)
    win = jnp.concatenate(
        [jnp.full((1, POOL_GROUP), float(w), jnp.float32) for w in POOL_WINDOWS], axis=-1)
    pos_top = (pos_base + lax.broadcasted_iota(jnp.int32, (POOL_HIST_ROWS, 1), 0)).astype(jnp.float32)
    cnt_top = jnp.minimum(pos_top + 1.0, win)

    x3 = x_ref[...].reshape(nseg, L, D_MODEL)
    for l in range(depth):
        shift3, scale3, gate3 = mod_ref[l, 0], mod_ref[l, 1], mod_ref[l, 2]
        amp3 = norm_g_ref[l][None] * (1.0 + scale3)
        ms = jnp.mean(x3 * x3, axis=-1, keepdims=True)
        h3 = (x3 * lax.rsqrt(ms + EPS)) * amp3 + shift3
        hb = h3.reshape(R, D_MODEL).astype(jnp.bfloat16)
        z = jnp.dot(hb, w_in_ref[l], preferred_element_type=jnp.float32)
        p, zp, gb, gc, hx, zc, u, v, zg = (z[:, _OFFS[i]:_OFFS[i + 1]] for i in range(9))

        p3 = p.reshape(nseg, L, POOL_W)
        e = _ext_rows(pool_in_ref[l], p3)
        s2 = e + pltpu.roll(e, 1, 0)
        s4 = s2 + pltpu.roll(s2, 2, 0)
        s4b = s4[:, LANES:]
        s8 = s4b + pltpu.roll(s4b, 4, 0)
        s16 = s8 + pltpu.roll(s8, 8, 0)
        wsum = jnp.concatenate([jnp.where(low_half, s2[:, :LANES], s4[:, :LANES]),
                                jnp.where(low_half, s8, s16)], axis=-1)
        wsum3 = _unext(wsum, nseg, POOL_HIST_ROWS, L)
        d_top = wsum3[:, :POOL_HIST_ROWS] / cnt_top[None] - p3[:, :POOL_HIST_ROWS]
        if L > POOL_HIST_ROWS:
            d_rest = wsum3[:, POOL_HIST_ROWS:] * inv_w[None] - p3[:, POOL_HIST_ROWS:]
            d3 = jnp.concatenate([d_top, d_rest], axis=1)
        else:
            d3 = d_top
        pooled = jnp.dot(d3.reshape(R, POOL_W).astype(jnp.bfloat16), wpool_ref[l],
                         preferred_element_type=jnp.float32) * pscale_ref[l]
        y_pool = pooled * _silu(zp)
        pool_out_ref[l] = p3[:, L - POOL_HIST_ROWS:, :]

        q3 = (gc * hx).reshape(nseg, L, CONV_W)
        qe = _ext_rows(conv_in_ref[l], q3)
        wc = wconv_ref[l]
        conv = wc[2:3] * qe + wc[1:2] * pltpu.roll(qe, 1, 0) + wc[0:1] * pltpu.roll(qe, 2, 0)
        conv = _unext(conv, nseg, CONV_HIST_ROWS, L).reshape(R, CONV_W)
        y_conv = gb * conv * _silu(zc)
        conv_out_ref[l] = q3[:, L - CONV_HIST_ROWS:, :]

        mu = jnp.mean(v, axis=-1, keepdims=True)
        vc = v - mu
        var = jnp.mean(vc * vc, axis=-1, keepdims=True)
        vn = (vc * lax.rsqrt(var + EPS)) * vng_ref[l] + vnb_ref[l]
        if vs_ref is not None:
            vs_ref[l] = vn.reshape(nseg, L, GMLP_W)
        mixed_chunks = []
        for c in range(R // gchunk):
            cols = []
            for j in range(GMLP_W // LANES):
                vcj = vn[c * gchunk:(c + 1) * gchunk, j * LANES:(j + 1) * LANES]
                rhs = jnp.concatenate([jnp.where(low_half, vcj, 0.0), jnp.where(low_half, 0.0, vcj)],
                                      axis=0).astype(jnp.bfloat16)
                cols.append(jnp.dot(glhs_ref[l, j], rhs, preferred_element_type=jnp.float32))
            mixed_chunks.append(jnp.concatenate(cols, axis=-1) + gbias_ref[l])
        mixed = mixed_chunks[0] if len(mixed_chunks) == 1 else jnp.concatenate(mixed_chunks, axis=0)
        y_gmlp = (u * mixed) * _silu(zg)

        ycat = jnp.concatenate([y_pool, y_conv, y_gmlp], axis=-1).astype(jnp.bfloat16)
        y = jnp.dot(ycat, w_out_ref[l], preferred_element_type=jnp.float32)
        x3 = x3 + gate3 * y.reshape(nseg, L, D_MODEL)

    ms = jnp.mean(x3 * x3, axis=-1, keepdims=True)
    y_ref[...] = ((x3 * lax.rsqrt(ms + EPS)) * fin_g_ref[...][None]).reshape(y_ref.shape)


def _const_spec(shape):
    nd = len(shape)
    return pl.BlockSpec(shape, lambda *_: (0,) * nd, pipeline_mode=pl.Buffered(1))


def _gating_operands(w_s, b_s, n, reps):
    depth = w_s.shape[0]
    blk = jnp.arange(n) // CHUNK
    mask = blk[None, :] <= blk[:, None]
    w = jnp.where(mask[None, None], w_s[:, :, :n, :n], 0.0)
    if reps > 1:
        eye = jnp.eye(reps, dtype=w.dtype)
        w = jnp.einsum('ab,lhts->lhatbs', eye, w).reshape(depth, GMLP_HEADS, reps * n, reps * n)
    lhs = w.reshape(depth, GMLP_HEADS // 2, 2, reps * n, reps * n)
    lhs = jnp.concatenate([lhs[:, :, 0], lhs[:, :, 1]], axis=-1).astype(jnp.bfloat16)
    bias = jnp.repeat(jnp.swapaxes(b_s[:, :, :n], 1, 2), GMLP_HEAD_DIM, axis=-1)
    bias = jnp.tile(bias, (1, reps, 1))
    return lhs, bias


def kernel(x_prompt, x_sample, state_pool, state_conv, c_prompt, c_sample, norm_g, w_ada, b_ada, w_in, w_pool, pool_scale, w_conv, v_norm_g, v_norm_b, w_s, b_s, w_out, final_norm_g):
    depth = w_in.shape[0]
    nb, seq, d = x_prompt.shape
    ndec, dec_seq, _ = x_sample.shape
    f32 = jnp.float32

    c_all = jnp.concatenate([c_prompt, c_sample], axis=0)
    nreq = nb + ndec
    mod = pl.pallas_call(
        _mod_kernel,
        grid=(depth, 3),
        in_specs=[pl.BlockSpec((nreq, d), lambda l, k: (0, 0)),
                  pl.BlockSpec((1, d, d), lambda l, k: (l, 0, k)),
                  pl.BlockSpec((1, 1, d), lambda l, k: (l, 0, k))],
        out_specs=pl.BlockSpec((1, 1, nreq, d), lambda l, k: (l, k, 0, 0)),
        out_shape=jax.ShapeDtypeStruct((depth, 3, nreq, d), f32),
        name="adaln_mod",
    )(c_all, w_ada, b_ada.reshape(depth, 1, 3 * d))
    mod = mod.reshape(depth, 3, nreq, 1, d)
    mod_p, mod_s = mod[:, :, :nb], mod[:, :, nb:]

    w_in_b = w_in.astype(jnp.bfloat16)
    w_out_b = w_out.astype(jnp.bfloat16)
    ngroups = len(POOL_WINDOWS)
    wpool_bd = jnp.einsum('ab,lade->ladbe', jnp.eye(ngroups, dtype=f32), w_pool)
    wpool_bd = wpool_bd.reshape(depth, POOL_W, POOL_W).astype(jnp.bfloat16)
    norm_g3 = norm_g.reshape(depth, 1, d)
    pscale3 = pool_scale.reshape(depth, 1, POOL_W)
    vng3 = v_norm_g.reshape(depth, 1, GMLP_W)
    vnb3 = v_norm_b.reshape(depth, 1, GMLP_W)
    fin_g2 = final_norm_g.reshape(1, d)

    tl = min(SEQ_TILE, seq)
    assert seq % tl == 0 and tl % GMLP_CHUNK == 0
    glhs_p, gbias_p = _gating_operands(w_s, b_s, GMLP_CHUNK, 1)
    shared = (norm_g3, w_in_b, wpool_bd, pscale3, w_conv, vng3, vnb3)
    tail = (w_out_b, fin_g2)
    prompt_in = (x_prompt, mod_p) + shared + (glhs_p, gbias_p) + tail
    prompt_specs = [pl.BlockSpec((1, tl, d), lambda b, t: (b, t, 0)),
                    pl.BlockSpec((depth, 3, 1, 1, d), lambda b, t: (0, 0, b, 0, 0))]
    prompt_specs += [_const_spec(a.shape) for a in prompt_in[2:]]
    y_prompt, pool_p, conv_p = pl.pallas_call(
        functools.partial(_trunk_kernel, nseg=1, rows=tl, gchunk=GMLP_CHUNK, carried=True, pos0=0),
        grid=(nb, seq // tl),
        in_specs=prompt_specs,
        out_specs=[pl.BlockSpec((1, tl, d), lambda b, t: (b, t, 0)),
                   pl.BlockSpec((depth, 1, POOL_HIST_ROWS, POOL_W), lambda b, t: (0, b, 0, 0)),
                   pl.BlockSpec((depth, 1, CONV_HIST_ROWS, CONV_W), lambda b, t: (0, b, 0, 0))],
        out_shape=[jax.ShapeDtypeStruct((nb, seq, d), f32),
                   jax.ShapeDtypeStruct((depth, nb, POOL_HIST_ROWS, POOL_W), f32),
                   jax.ShapeDtypeStruct((depth, nb, CONV_HIST_ROWS, CONV_W), f32)],
        compiler_params=pltpu.CompilerParams(
            dimension_semantics=("arbitrary", "arbitrary"), vmem_limit_bytes=VMEM_LIMIT_BYTES),
        name="prompt_trunk",
    )(*prompt_in)

    assert dec_seq >= POOL_HIST_ROWS and dec_seq % 8 == 0 and dec_seq <= CHUNK
    glhs_s, gbias_s = _gating_operands(w_s, b_s, dec_seq, ndec)
    pool_state = jnp.pad(state_pool, ((0, 0), (0, 0), (POOL_HIST_ROWS - state_pool.shape[2], 0), (0, 0)))
    conv_state = jnp.pad(state_conv, ((0, 0), (0, 0), (CONV_HIST_ROWS - state_conv.shape[2], 0), (0, 0)))
    sample_in = (x_sample, mod_s) + shared + (glhs_s, gbias_s) + tail + (pool_state, conv_state)
    y_sample, pool_s, conv_s, v_s = pl.pallas_call(
        functools.partial(_trunk_kernel, nseg=ndec, rows=dec_seq, gchunk=ndec * dec_seq,
                          carried=False, pos0=PAST_LEN),
        out_shape=[jax.ShapeDtypeStruct((ndec, dec_seq, d), f32),
                   jax.ShapeDtypeStruct((depth, ndec, POOL_HIST_ROWS, POOL_W), f32),
                   jax.ShapeDtypeStruct((depth, ndec, CONV_HIST_ROWS, CONV_W), f32),
                   jax.ShapeDtypeStruct((depth, ndec, dec_seq, GMLP_W), f32)],
        compiler_params=pltpu.CompilerParams(vmem_limit_bytes=VMEM_LIMIT_BYTES),
        name="sample_trunk",
    )(*sample_in)

    npool = state_pool.shape[2]
    nconv = state_conv.shape[2]
    return (y_prompt, y_sample,
            pool_p[:, :, POOL_HIST_ROWS - npool:], conv_p[:, :, CONV_HIST_ROWS - nconv:],
            pool_s[:, :, POOL_HIST_ROWS - npool:], conv_s[:, :, CONV_HIST_ROWS - nconv:], v_s)
```

```python
import functools

import numpy as np
import jax
import jax.numpy as jnp
from jax import lax
from jax.experimental import pallas as pl
from jax.experimental.pallas import tpu as pltpu

D_MODEL = 1024
POOL_WINDOWS = (2, 4, 8, 16)
POOL_GROUP = 64
POOL_W = 256
CONV_W = 384
CONV_K = 3
GMLP_W = 384
GMLP_HEADS = 6
GMLP_HEAD_DIM = 64
GMLP_CHUNK = 128
CHUNK = 64
PROJ_W = 3200
PAST_LEN = 1024
EPS = 1e-6

_SPLITS = (256, 256, 384, 384, 384, 384, 384, 384, 384)
_OFFS = tuple(int(v) for v in np.cumsum((0,) + _SPLITS))

SUBLANES = 8
LANES = 128
POOL_HIST_ROWS = 2 * SUBLANES
CONV_HIST_ROWS = SUBLANES
SEQ_TILE = 1024
VMEM_LIMIT_BYTES = 56 * 1024 * 1024


def _silu(z):
    return z * (1.0 / (1.0 + jnp.exp(-z)))


def _mod_kernel(cp_ref, cs_ref, w_ref, b_ref, op_ref, os_ref):
    nb = cp_ref.shape[0]
    layer = pl.program_id(0)
    c = jnp.concatenate([cp_ref[...], cs_ref[...]], axis=0)
    mod = jnp.dot(_silu(c), w_ref[0], preferred_element_type=jnp.float32) + b_ref[pl.ds(layer, 1), :]
    op_ref[0, 0] = mod[:nb]
    os_ref[0, 0] = mod[nb:]


def _ext_rows(hist3, val2):
    nseg, h, c = hist3.shape
    l = val2.shape[0] // nseg
    return jnp.concatenate([hist3, val2.reshape(nseg, l, c)], axis=1).reshape(nseg * (h + l), c)


def _unext(e2, nseg, h):
    l = e2.shape[0] // nseg - h
    return e2.reshape(nseg, h + l, e2.shape[-1])[:, h:, :].reshape(nseg * l, e2.shape[-1])


def _trunk_kernel(*refs, nseg, rows, gchunk, carried, pos0):
    (x_ref, mod_ref, norm_g_ref, w_in_ref, wpool_ref, pscale_ref, wconv_ref, vng_ref, vnb_ref,
     glhs_ref, gbias_ref, w_out_ref, fin_g_ref) = refs[:13]
    L = rows
    R = nseg * L
    depth = w_in_ref.shape[0]
    npool = POOL_HIST_ROWS - 1
    nconv = CONV_K - 1
    if carried:
        y_ref, pool_out_ref, conv_out_ref, pool_hist, conv_hist = refs[13:]
        vs_ref = None
        tile = pl.program_id(1)
        pos_base = tile * rows + pos0
        mod_row = pl.program_id(0) % SUBLANES

        @pl.when(tile == 0)
        def _():
            pool_hist[...] = jnp.zeros_like(pool_hist)
            conv_hist[...] = jnp.zeros_like(conv_hist)

        def mod_rows(l, k):
            return mod_ref[l, k, pl.ds(mod_row, 1), :]
    else:
        pool_in_ref, conv_in_ref = refs[13:15]
        y_ref, pool_out_ref, conv_out_ref, vs_ref, pool_hist, conv_hist = refs[15:]
        pos_base = pos0
        pool_hist[:, :, 0:1, :] = jnp.zeros((depth, nseg, 1, POOL_W), jnp.float32)
        pool_hist[:, :, 1:, :] = pool_in_ref[...]
        conv_hist[:, :, :CONV_HIST_ROWS - nconv, :] = jnp.zeros(
            (depth, nseg, CONV_HIST_ROWS - nconv, CONV_W), jnp.float32)
        conv_hist[:, :, CONV_HIST_ROWS - nconv:, :] = conv_in_ref[...]

        def mod_rows(l, k):
            m = mod_ref[l, k]
            return jnp.concatenate(
                [jnp.broadcast_to(m[i:i + 1], (L, D_MODEL)) for i in range(nseg)], axis=0)

    lane = lax.broadcasted_iota(jnp.int32, (1, LANES), 1)
    low_half = lane < POOL_GROUP

    inv_w = jnp.concatenate(
        [jnp.full((1, POOL_GROUP), 1.0 / w, jnp.float32) for w in POOL_WINDOWS], axis=-1)
    win = jnp.concatenate(
        [jnp.full((1, POOL_GROUP), float(w), jnp.float32) for w in POOL_WINDOWS], axis=-1)
    pos_top = (pos_base + lax.broadcasted_iota(jnp.int32, (POOL_HIST_ROWS, 1), 0)).astype(jnp.float32)
    cnt_top = jnp.minimum(pos_top + 1.0, win)

    x = x_ref[...].reshape(R, D_MODEL)
    for l in range(depth):
        shift, scale, gate = mod_rows(l, 0), mod_rows(l, 1), mod_rows(l, 2)
        amp = norm_g_ref[l:l + 1, :] * (1.0 + scale)
        ms = jnp.mean(x * x, axis=-1, keepdims=True)
        hb = ((x * lax.rsqrt(ms + EPS)) * amp + shift).astype(jnp.bfloat16)
        z = jnp.dot(hb, w_in_ref[l], preferred_element_type=jnp.float32)
        p, zp, gb, gc, hx, zc, u, v, zg = (z[:, _OFFS[i]:_OFFS[i + 1]] for i in range(9))

        e = _ext_rows(pool_hist[l], p)
        s2 = e + pltpu.roll(e, 1, 0)
        s4 = s2 + pltpu.roll(s2, 2, 0)
        s4b = s4[:, LANES:]
        s8 = s4b + pltpu.roll(s4b, 4, 0)
        s16 = s8 + pltpu.roll(s8, 8, 0)
        wsum = jnp.concatenate([jnp.where(low_half, s2[:, :LANES], s4[:, :LANES]),
                                jnp.where(low_half, s8, s16)], axis=-1)
        wsum3 = _unext(wsum, nseg, POOL_HIST_ROWS).reshape(nseg, L, POOL_W)
        p3 = p.reshape(nseg, L, POOL_W)
        d_top = wsum3[:, :POOL_HIST_ROWS] / cnt_top[None] - p3[:, :POOL_HIST_ROWS]
        if L > POOL_HIST_ROWS:
            d_rest = wsum3[:, POOL_HIST_ROWS:] * inv_w[None] - p3[:, POOL_HIST_ROWS:]
            d3 = jnp.concatenate([d_top, d_rest], axis=1)
        else:
            d3 = d_top
        pooled = jnp.dot(d3.reshape(R, POOL_W).astype(jnp.bfloat16), wpool_ref[l],
                         preferred_element_type=jnp.float32) * pscale_ref[l:l + 1, :]
        y_pool = pooled * _silu(zp)
        pool_out_ref[l] = p3[:, L - npool:, :].reshape(pool_out_ref.shape[1:])
        if carried:
            pool_hist[l] = p3[:, L - POOL_HIST_ROWS:, :]

        q = gc * hx
        qe = _ext_rows(conv_hist[l], q)
        wc = wconv_ref[l]
        conv = wc[2:3] * qe + wc[1:2] * pltpu.roll(qe, 1, 0) + wc[0:1] * pltpu.roll(qe, 2, 0)
        y_conv = gb * _unext(conv, nseg, CONV_HIST_ROWS) * _silu(zc)
        q3 = q.reshape(nseg, L, CONV_W)
        conv_out_ref[l] = q3[:, L - nconv:, :].reshape(conv_out_ref.shape[1:])
        if carried:
            conv_hist[l] = q3[:, L - CONV_HIST_ROWS:, :]

        mu = jnp.mean(v, axis=-1, keepdims=True)
        vc = v - mu
        var = jnp.mean(vc * vc, axis=-1, keepdims=True)
        vn = (vc * lax.rsqrt(var + EPS)) * vng_ref[l:l + 1, :] + vnb_ref[l:l + 1, :]
        if vs_ref is not None:
            vs_ref[l] = vn.reshape(nseg, L, GMLP_W)
        mixed_chunks = []
        for c in range(R // gchunk):
            cols = []
            for j in range(GMLP_W // LANES):
                vcj = vn[c * gchunk:(c + 1) * gchunk, j * LANES:(j + 1) * LANES]
                rhs = jnp.concatenate([jnp.where(low_half, vcj, 0.0), jnp.where(low_half, 0.0, vcj)],
                                      axis=0).astype(jnp.bfloat16)
                cols.append(jnp.dot(glhs_ref[l, j], rhs, preferred_element_type=jnp.float32))
            mixed_chunks.append(jnp.concatenate(cols, axis=-1) + gbias_ref[l])
        mixed = mixed_chunks[0] if len(mixed_chunks) == 1 else jnp.concatenate(mixed_chunks, axis=0)
        y_gmlp = (u * mixed) * _silu(zg)

        ycat = jnp.concatenate([y_pool, y_conv, y_gmlp], axis=-1).astype(jnp.bfloat16)
        x = x + gate * jnp.dot(ycat, w_out_ref[l], preferred_element_type=jnp.float32)

    ms = jnp.mean(x * x, axis=-1, keepdims=True)
    y_ref[...] = ((x * lax.rsqrt(ms + EPS)) * fin_g_ref[...]).reshape(y_ref.shape)


def _const_spec(shape):
    nd = len(shape)
    return pl.BlockSpec(shape, lambda *_: (0,) * nd, pipeline_mode=pl.Buffered(1))


def _gating_operands(w_s, b_s, n):
    blk = jnp.arange(n) // CHUNK
    mask = blk[None, :] <= blk[:, None]
    w = jnp.where(mask[None, None], w_s[:, :, :n, :n], 0.0).astype(jnp.bfloat16)
    lhs = jnp.concatenate([w[:, 0::2], w[:, 1::2]], axis=-1)
    bias = jnp.repeat(jnp.swapaxes(b_s[:, :, :n], 1, 2), GMLP_HEAD_DIM, axis=-1)
    return lhs, bias


def kernel(x_prompt, x_sample, state_pool, state_conv, c_prompt, c_sample, norm_g, w_ada, b_ada, w_in, w_pool, pool_scale, w_conv, v_norm_g, v_norm_b, w_s, b_s, w_out, final_norm_g):
    depth = w_in.shape[0]
    nb, seq, d = x_prompt.shape
    ndec, dec_seq, _ = x_sample.shape
    npool, nconv = state_pool.shape[2], state_conv.shape[2]
    assert npool == POOL_HIST_ROWS - 1 and nconv == CONV_K - 1 and d == D_MODEL
    f32 = jnp.float32

    mod_p, mod_s = pl.pallas_call(
        _mod_kernel,
        grid=(depth, 3),
        in_specs=[pl.BlockSpec((nb, d), lambda l, k: (0, 0)),
                  pl.BlockSpec((ndec, d), lambda l, k: (0, 0)),
                  pl.BlockSpec((1, d, d), lambda l, k: (l, 0, k)),
                  pl.BlockSpec((depth, d), lambda l, k: (0, k))],
        out_specs=[pl.BlockSpec((1, 1, nb, d), lambda l, k: (l, k, 0, 0)),
                   pl.BlockSpec((1, 1, ndec, d), lambda l, k: (l, k, 0, 0))],
        out_shape=[jax.ShapeDtypeStruct((depth, 3, nb, d), f32),
                   jax.ShapeDtypeStruct((depth, 3, ndec, d), f32)],
        name="adaln_mod",
    )(c_prompt, c_sample, w_ada, b_ada)

    w_in_b = w_in.astype(jnp.bfloat16)
    w_out_b = w_out.astype(jnp.bfloat16)
    ngroups = len(POOL_WINDOWS)
    wpool_bd = jnp.einsum('ab,lade->ladbe', jnp.eye(ngroups, dtype=f32), w_pool)
    wpool_bd = wpool_bd.reshape(depth, POOL_W, POOL_W).astype(jnp.bfloat16)
    fin_g2 = final_norm_g.reshape(1, d)
    shared = (norm_g, w_in_b, wpool_bd, pool_scale, w_conv, v_norm_g, v_norm_b)
    tail = (w_out_b, fin_g2)

    tl = min(SEQ_TILE, seq)
    assert seq % tl == 0 and tl % GMLP_CHUNK == 0 and nb % SUBLANES == 0
    glhs_p, gbias_p = _gating_operands(w_s, b_s, GMLP_CHUNK)
    prompt_in = (x_prompt, mod_p) + shared + (glhs_p, gbias_p) + tail
    prompt_specs = [pl.BlockSpec((1, tl, d), lambda b, t: (b, t, 0)),
                    pl.BlockSpec((depth, 3, SUBLANES, d), lambda b, t: (0, 0, b // SUBLANES, 0))]
    prompt_specs += [_const_spec(a.shape) for a in prompt_in[2:]]
    y_prompt, pool_p, conv_p = pl.pallas_call(
        functools.partial(_trunk_kernel, nseg=1, rows=tl, gchunk=GMLP_CHUNK, carried=True, pos0=0),
        grid=(nb, seq // tl),
        in_specs=prompt_specs,
        out_specs=[pl.BlockSpec((1, tl, d), lambda b, t: (b, t, 0)),
                   pl.BlockSpec((depth, 1, npool, POOL_W), lambda b, t: (0, b, 0, 0)),
                   pl.BlockSpec((depth, 1, nconv, CONV_W), lambda b, t: (0, b, 0, 0))],
        out_shape=[jax.ShapeDtypeStruct((nb, seq, d), f32),
                   jax.ShapeDtypeStruct((depth, nb, npool, POOL_W), f32),
                   jax.ShapeDtypeStruct((depth, nb, nconv, CONV_W), f32)],
        scratch_shapes=[pltpu.VMEM((depth, 1, POOL_HIST_ROWS, POOL_W), f32),
                        pltpu.VMEM((depth, 1, CONV_HIST_ROWS, CONV_W), f32)],
        compiler_params=pltpu.CompilerParams(
            dimension_semantics=("arbitrary", "arbitrary"), vmem_limit_bytes=VMEM_LIMIT_BYTES),
        name="prompt_trunk",
    )(*prompt_in)

    assert dec_seq >= POOL_HIST_ROWS and dec_seq % (2 * SUBLANES) == 0 and dec_seq <= CHUNK
    glhs_s, gbias_s = _gating_operands(w_s, b_s, dec_seq)
    sample_in = (x_sample, mod_s) + shared + (glhs_s, gbias_s) + tail + (state_pool, state_conv)
    y_sample, pool_s, conv_s, v_s = pl.pallas_call(
        functools.partial(_trunk_kernel, nseg=ndec, rows=dec_seq, gchunk=dec_seq,
                          carried=False, pos0=PAST_LEN),
        out_shape=[jax.ShapeDtypeStruct((ndec, dec_seq, d), f32),
                   jax.ShapeDtypeStruct((depth, ndec, npool, POOL_W), f32),
                   jax.ShapeDtypeStruct((depth, ndec, nconv, CONV_W), f32),
                   jax.ShapeDtypeStruct((depth, ndec, dec_seq, GMLP_W), f32)],
        scratch_shapes=[pltpu.VMEM((depth, ndec, POOL_HIST_ROWS, POOL_W), f32),
                        pltpu.VMEM((depth, ndec, CONV_HIST_ROWS, CONV_W), f32)],
        compiler_params=pltpu.CompilerParams(vmem_limit_bytes=VMEM_LIMIT_BYTES),
        name="sample_trunk",
    )(*sample_in)

    return y_prompt, y_sample, pool_p, conv_p, pool_s, conv_s, v_s
```

```python
import functools

import numpy as np
import jax
import jax.numpy as jnp
from jax import lax
from jax.experimental import pallas as pl
from jax.experimental.pallas import tpu as pltpu

D_MODEL = 1024
POOL_WINDOWS = (2, 4, 8, 16)
POOL_GROUP = 64
POOL_W = 256
CONV_W = 384
CONV_K = 3
GMLP_W = 384
GMLP_HEADS = 6
GMLP_HEAD_DIM = 64
GMLP_CHUNK = 128
CHUNK = 64
PROJ_W = 3200
PAST_LEN = 1024
EPS = 1e-6

_SPLITS = (256, 256, 384, 384, 384, 384, 384, 384, 384)
_OFFS = tuple(int(v) for v in np.cumsum((0,) + _SPLITS))

SUBLANES = 8
LANES = 128
POOL_HIST_ROWS = 2 * SUBLANES
CONV_HIST_ROWS = SUBLANES
SEQ_TILE = 1024
SEQ_SUBTILES = 1
GMLP_CHUNKS_PER_DOT = 1
OUT_ROW_BLOCK = 256
VMEM_LIMIT_BYTES = 56 * 1024 * 1024


def _silu(z):
    return z * (1.0 / (1.0 + jnp.exp(-z)))


def _mod_kernel(cp_ref, cs_ref, w_ref, b_ref, op_ref, os_ref):
    nb = cp_ref.shape[0]
    layer = pl.program_id(0)
    c = jnp.concatenate([cp_ref[...], cs_ref[...]], axis=0)
    mod = jnp.dot(_silu(c), w_ref[0], preferred_element_type=jnp.float32) + b_ref[pl.ds(layer, 1), :]
    op_ref[0, 0] = mod[:nb]
    os_ref[0, 0] = mod[nb:]


def _ext_rows(hist3, val2):
    nseg, h, c = hist3.shape
    l = val2.shape[0] // nseg
    return jnp.concatenate([hist3, val2.reshape(nseg, l, c)], axis=1).reshape(nseg * (h + l), c)


def _unext(e2, nseg, h):
    l = e2.shape[0] // nseg - h
    return e2.reshape(nseg, h + l, e2.shape[-1])[:, h:, :].reshape(nseg * l, e2.shape[-1])


def _trunk_kernel(*refs, nseg, rows, nsub, gchunk, carried, pos0):
    (x_ref, mod_ref, norm_g_ref, w_in_ref, wpool_ref, pscale_ref, wconv_ref, vng_ref, vnb_ref,
     glhs_ref, gbias_ref, w_out_ref, fin_g_ref) = refs[:13]
    L = rows
    R = nseg * L
    depth = w_in_ref.shape[0]
    npool = POOL_HIST_ROWS - 1
    nconv = CONV_K - 1
    if carried:
        y_ref, pool_out_ref, conv_out_ref, pool_hist, conv_hist = refs[13:]
        vs_ref = None
        tile = pl.program_id(1)
        mod_row = pl.program_id(0) % SUBLANES

        @pl.when(tile == 0)
        def _():
            pool_hist[...] = jnp.zeros_like(pool_hist)
            conv_hist[...] = jnp.zeros_like(conv_hist)

        def mod_rows(l, k):
            return mod_ref[l, k, pl.ds(mod_row, 1), :]
    else:
        pool_in_ref, conv_in_ref = refs[13:15]
        y_ref, pool_out_ref, conv_out_ref, vs_ref, pool_hist, conv_hist = refs[15:]
        tile = 0
        pool_hist[:, :, 0:1, :] = jnp.zeros((depth, nseg, 1, POOL_W), jnp.float32)
        pool_hist[:, :, 1:, :] = pool_in_ref[...]
        conv_hist[:, :, :CONV_HIST_ROWS - nconv, :] = jnp.zeros(
            (depth, nseg, CONV_HIST_ROWS - nconv, CONV_W), jnp.float32)
        conv_hist[:, :, CONV_HIST_ROWS - nconv:, :] = conv_in_ref[...]

        def mod_rows(l, k):
            m = mod_ref[l, k]
            return jnp.concatenate(
                [jnp.broadcast_to(m[i:i + 1], (L, D_MODEL)) for i in range(nseg)], axis=0)

    lane = lax.broadcasted_iota(jnp.int32, (1, LANES), 1)
    low_half = lane < POOL_GROUP

    inv_w = jnp.concatenate(
        [jnp.full((1, POOL_GROUP), 1.0 / w, jnp.float32) for w in POOL_WINDOWS], axis=-1)
    win = jnp.concatenate(
        [jnp.full((1, POOL_GROUP), float(w), jnp.float32) for w in POOL_WINDOWS], axis=-1)

    def layer(l, x, cnt_top):
        shift, scale, gate = mod_rows(l, 0), mod_rows(l, 1), mod_rows(l, 2)
        amp = norm_g_ref[l:l + 1, :] * (1.0 + scale)
        ms = jnp.mean(x * x, axis=-1, keepdims=True)
        hb = ((x * lax.rsqrt(ms + EPS)) * amp + shift).astype(jnp.bfloat16)
        z = jnp.dot(hb, w_in_ref[l], preferred_element_type=jnp.float32)
        p, zp, gb, gc, hx, zc, u, v, zg = (z[:, _OFFS[i]:_OFFS[i + 1]] for i in range(9))

        e = _ext_rows(pool_hist[l], p)
        s2 = e + pltpu.roll(e, 1, 0)
        s4 = s2 + pltpu.roll(s2, 2, 0)
        s4b = s4[:, LANES:]
        s8 = s4b + pltpu.roll(s4b, 4, 0)
        s16 = s8 + pltpu.roll(s8, 8, 0)
        wsum = jnp.concatenate([jnp.where(low_half, s2[:, :LANES], s4[:, :LANES]),
                                jnp.where(low_half, s8, s16)], axis=-1)
        wsum3 = _unext(wsum, nseg, POOL_HIST_ROWS).reshape(nseg, L, POOL_W)
        p3 = p.reshape(nseg, L, POOL_W)
        d_top = wsum3[:, :POOL_HIST_ROWS] / cnt_top[None] - p3[:, :POOL_HIST_ROWS]
        if L > POOL_HIST_ROWS:
            d_rest = wsum3[:, POOL_HIST_ROWS:] * inv_w[None] - p3[:, POOL_HIST_ROWS:]
            d3 = jnp.concatenate([d_top, d_rest], axis=1)
        else:
            d3 = d_top
        pooled = jnp.dot(d3.reshape(R, POOL_W).astype(jnp.bfloat16), wpool_ref[l],
                         preferred_element_type=jnp.float32) * pscale_ref[l:l + 1, :]
        y_pool = pooled * _silu(zp)
        pool_out_ref[l] = p3[:, L - npool:, :].reshape(pool_out_ref.shape[1:])
        if carried:
            pool_hist[l] = p3[:, L - POOL_HIST_ROWS:, :]

        q = gc * hx
        qe = _ext_rows(conv_hist[l], q)
        wc = wconv_ref[l]
        conv = wc[2:3] * qe + wc[1:2] * pltpu.roll(qe, 1, 0) + wc[0:1] * pltpu.roll(qe, 2, 0)
        y_conv = gb * _unext(conv, nseg, CONV_HIST_ROWS) * _silu(zc)
        q3 = q.reshape(nseg, L, CONV_W)
        conv_out_ref[l] = q3[:, L - nconv:, :].reshape(conv_out_ref.shape[1:])
        if carried:
            conv_hist[l] = q3[:, L - CONV_HIST_ROWS:, :]

        mu = jnp.mean(v, axis=-1, keepdims=True)
        vc = v - mu
        var = jnp.mean(vc * vc, axis=-1, keepdims=True)
        vn = (vc * lax.rsqrt(var + EPS)) * vng_ref[l:l + 1, :] + vnb_ref[l:l + 1, :]
        if vs_ref is not None:
            vs_ref[l] = vn.reshape(nseg, L, GMLP_W)
        nchunk = R // gchunk
        cpd = GMLP_CHUNKS_PER_DOT if nchunk % GMLP_CHUNKS_PER_DOT == 0 else 1
        mixed_chunks = [[] for _ in range(nchunk)]
        for c in range(0, nchunk, cpd):
            for j in range(GMLP_W // LANES):
                rhs = []
                for cc in range(c, c + cpd):
                    vcj = vn[cc * gchunk:(cc + 1) * gchunk, j * LANES:(j + 1) * LANES]
                    rhs.append(jnp.concatenate(
                        [jnp.where(low_half, vcj, 0.0), jnp.where(low_half, 0.0, vcj)], axis=0))
                rhs = jnp.concatenate(rhs, axis=-1).astype(jnp.bfloat16)
                m = jnp.dot(glhs_ref[l, j], rhs, preferred_element_type=jnp.float32)
                for k in range(cpd):
                    mixed_chunks[c + k].append(m[:, k * LANES:(k + 1) * LANES])
        mixed = jnp.concatenate(
            [jnp.concatenate(cols, axis=-1) + gbias_ref[l] for cols in mixed_chunks], axis=0)
        y_gmlp = (u * mixed) * _silu(zg)

        ycat = jnp.concatenate([y_pool, y_conv, y_gmlp], axis=-1).astype(jnp.bfloat16)
        rb = min(OUT_ROW_BLOCK, R)
        y = jnp.concatenate(
            [jnp.dot(ycat[r:r + rb], w_out_ref[l], preferred_element_type=jnp.float32)
             for r in range(0, R, rb)], axis=0)
        return x + gate * y

    for s in range(nsub):
        pos_base = (tile * nsub + s) * L + pos0
        pos_top = (pos_base + lax.broadcasted_iota(jnp.int32, (POOL_HIST_ROWS, 1), 0)).astype(jnp.float32)
        cnt_top = jnp.minimum(pos_top + 1.0, win)
        x = x_ref[:, s * L:(s + 1) * L, :].reshape(R, D_MODEL)
        for l in range(depth):
            x = layer(l, x, cnt_top)
        ms = jnp.mean(x * x, axis=-1, keepdims=True)
        y_ref[:, s * L:(s + 1) * L, :] = ((x * lax.rsqrt(ms + EPS)) * fin_g_ref[...]).reshape(nseg, L, D_MODEL)


def _const_spec(shape):
    nd = len(shape)
    return pl.BlockSpec(shape, lambda *_: (0,) * nd, pipeline_mode=pl.Buffered(1))


def _gating_operands(w_s, b_s, n):
    blk = jnp.arange(n) // CHUNK
    mask = blk[None, :] <= blk[:, None]
    w = jnp.where(mask[None, None], w_s[:, :, :n, :n], 0.0).astype(jnp.bfloat16)
    lhs = jnp.concatenate([w[:, 0::2], w[:, 1::2]], axis=-1)
    bias = jnp.repeat(jnp.swapaxes(b_s[:, :, :n], 1, 2), GMLP_HEAD_DIM, axis=-1)
    return lhs, bias


def kernel(x_prompt, x_sample, state_pool, state_conv, c_prompt, c_sample, norm_g, w_ada, b_ada, w_in, w_pool, pool_scale, w_conv, v_norm_g, v_norm_b, w_s, b_s, w_out, final_norm_g):
    depth = w_in.shape[0]
    nb, seq, d = x_prompt.shape
    ndec, dec_seq, _ = x_sample.shape
    npool, nconv = state_pool.shape[2], state_conv.shape[2]
    assert npool == POOL_HIST_ROWS - 1 and nconv == CONV_K - 1 and d == D_MODEL
    f32 = jnp.float32

    mod_p, mod_s = pl.pallas_call(
        _mod_kernel,
        grid=(depth, 3),
        in_specs=[pl.BlockSpec((nb, d), lambda l, k: (0, 0)),
                  pl.BlockSpec((ndec, d), lambda l, k: (0, 0)),
                  pl.BlockSpec((1, d, d), lambda l, k: (l, 0, k)),
                  pl.BlockSpec((depth, d), lambda l, k: (0, k))],
        out_specs=[pl.BlockSpec((1, 1, nb, d), lambda l, k: (l, k, 0, 0)),
                   pl.BlockSpec((1, 1, ndec, d), lambda l, k: (l, k, 0, 0))],
        out_shape=[jax.ShapeDtypeStruct((depth, 3, nb, d), f32),
                   jax.ShapeDtypeStruct((depth, 3, ndec, d), f32)],
        name="adaln_mod",
    )(c_prompt, c_sample, w_ada, b_ada)

    w_in_b = w_in.astype(jnp.bfloat16)
    w_out_b = w_out.astype(jnp.bfloat16)
    ngroups = len(POOL_WINDOWS)
    wpool_bd = jnp.einsum('ab,lade->ladbe', jnp.eye(ngroups, dtype=f32), w_pool)
    wpool_bd = wpool_bd.reshape(depth, POOL_W, POOL_W).astype(jnp.bfloat16)
    fin_g2 = final_norm_g.reshape(1, d)
    shared = (norm_g, w_in_b, wpool_bd, pool_scale, w_conv, v_norm_g, v_norm_b)
    tail = (w_out_b, fin_g2)

    tl = min(SEQ_TILE, seq)
    nsub = SEQ_SUBTILES if tl % (SEQ_SUBTILES * GMLP_CHUNK) == 0 else 1
    assert seq % tl == 0 and tl % GMLP_CHUNK == 0 and nb % SUBLANES == 0
    glhs_p, gbias_p = _gating_operands(w_s, b_s, GMLP_CHUNK)
    prompt_in = (x_prompt, mod_p) + shared + (glhs_p, gbias_p) + tail
    prompt_specs = [pl.BlockSpec((1, tl, d), lambda b, t: (b, t, 0)),
                    pl.BlockSpec((depth, 3, SUBLANES, d), lambda b, t: (0, 0, b // SUBLANES, 0))]
    prompt_specs += [_const_spec(a.shape) for a in prompt_in[2:]]
    y_prompt, pool_p, conv_p = pl.pallas_call(
        functools.partial(_trunk_kernel, nseg=1, rows=tl // nsub, nsub=nsub, gchunk=GMLP_CHUNK,
                          carried=True, pos0=0),
        grid=(nb, seq // tl),
        in_specs=prompt_specs,
        out_specs=[pl.BlockSpec((1, tl, d), lambda b, t: (b, t, 0)),
                   pl.BlockSpec((depth, 1, npool, POOL_W), lambda b, t: (0, b, 0, 0)),
                   pl.BlockSpec((depth, 1, nconv, CONV_W), lambda b, t: (0, b, 0, 0))],
        out_shape=[jax.ShapeDtypeStruct((nb, seq, d), f32),
                   jax.ShapeDtypeStruct((depth, nb, npool, POOL_W), f32),
                   jax.ShapeDtypeStruct((depth, nb, nconv, CONV_W), f32)],
        scratch_shapes=[pltpu.VMEM((depth, 1, POOL_HIST_ROWS, POOL_W), f32),
                        pltpu.VMEM((depth, 1, CONV_HIST_ROWS, CONV_W), f32)],
        compiler_params=pltpu.CompilerParams(
            dimension_semantics=("arbitrary", "arbitrary"), vmem_limit_bytes=VMEM_LIMIT_BYTES),
        name="prompt_trunk",
    )(*prompt_in)

    assert dec_seq >= POOL_HIST_ROWS and dec_seq % (2 * SUBLANES) == 0 and dec_seq <= CHUNK
    glhs_s, gbias_s = _gating_operands(w_s, b_s, dec_seq)
    sample_in = (x_sample, mod_s) + shared + (glhs_s, gbias_s) + tail + (state_pool, state_conv)
    y_sample, pool_s, conv_s, v_s = pl.pallas_call(
        functools.partial(_trunk_kernel, nseg=ndec, rows=dec_seq, nsub=1, gchunk=dec_seq,
                          carried=False, pos0=PAST_LEN),
        out_shape=[jax.ShapeDtypeStruct((ndec, dec_seq, d), f32),
                   jax.ShapeDtypeStruct((depth, ndec, npool, POOL_W), f32),
                   jax.ShapeDtypeStruct((depth, ndec, nconv, CONV_W), f32),
                   jax.ShapeDtypeStruct((depth, ndec, dec_seq, GMLP_W), f32)],
        scratch_shapes=[pltpu.VMEM((depth, ndec, POOL_HIST_ROWS, POOL_W), f32),
                        pltpu.VMEM((depth, ndec, CONV_HIST_ROWS, CONV_W), f32)],
        compiler_params=pltpu.CompilerParams(vmem_limit_bytes=VMEM_LIMIT_BYTES),
        name="sample_trunk",
    )(*sample_in)

    return y_prompt, y_sample, pool_p, conv_p, pool_s, conv_s, v_s
```

```python
import functools

import numpy as np
import jax
import jax.numpy as jnp
from jax import lax
from jax.experimental import pallas as pl
from jax.experimental.pallas import tpu as pltpu

D_MODEL = 1024
POOL_WINDOWS = (2, 4, 8, 16)
POOL_GROUP = 64
POOL_W = 256
CONV_W = 384
CONV_K = 3
GMLP_W = 384
GMLP_HEADS = 6
GMLP_HEAD_DIM = 64
GMLP_CHUNK = 128
CHUNK = 64
PROJ_W = 3200
PAST_LEN = 1024
EPS = 1e-6

_SPLITS = (256, 256, 384, 384, 384, 384, 384, 384, 384)
_OFFS = tuple(int(v) for v in np.cumsum((0,) + _SPLITS))

SUBLANES = 8
LANES = 128
POOL_HIST_ROWS = 2 * SUBLANES
CONV_HIST_ROWS = SUBLANES
SEQ_TILE = 1024
OUT_ROW_BLOCK = 256
VMEM_LIMIT_BYTES = 56 * 1024 * 1024


def _silu(z):
    return z * (1.0 / (1.0 + jnp.exp(-z)))


def _mod_kernel(cp_ref, cs_ref, w_ref, b_ref, op_ref, os_ref):
    nb = cp_ref.shape[0]
    layer = pl.program_id(0)
    c = jnp.concatenate([cp_ref[...], cs_ref[...]], axis=0)
    mod = jnp.dot(_silu(c), w_ref[0], preferred_element_type=jnp.float32) + b_ref[pl.ds(layer, 1), :]
    op_ref[0, 0] = mod[:nb]
    os_ref[0, 0] = mod[nb:]


def _ext_rows(hist3, val2):
    nseg, h, c = hist3.shape
    l = val2.shape[0] // nseg
    return jnp.concatenate([hist3, val2.reshape(nseg, l, c)], axis=1).reshape(nseg * (h + l), c)


def _unext(e2, nseg, h):
    l = e2.shape[0] // nseg - h
    return e2.reshape(nseg, h + l, e2.shape[-1])[:, h:, :].reshape(nseg * l, e2.shape[-1])


def _row(ref, l):
    return ref[pl.ds(l, 1), :]


def _trunk_kernel(*refs, nseg, rows, gchunk, streamed, pos0):
    (x_ref, mod_ref, norm_g_ref, w_in_ref, wpool_ref, pscale_ref, wconv_ref, vng_ref, vnb_ref,
     glhs_ref, gbias_ref, w_out_ref, fin_g_ref) = refs[:13]
    L = rows
    R = nseg * L
    depth = norm_g_ref.shape[0]
    npool = POOL_HIST_ROWS - 1
    nconv = CONV_K - 1
    if not streamed:
        y_ref, pool_out_ref, conv_out_ref, pool_hist, conv_hist = refs[13:]
        vs_ref = None
        w_in_b, w_out_b = w_in_ref, w_out_ref
        tile = pl.program_id(1)
        mod_row = pl.program_id(0) % SUBLANES

        @pl.when(tile == 0)
        def _():
            pool_hist[...] = jnp.zeros_like(pool_hist)
            conv_hist[...] = jnp.zeros_like(conv_hist)

        def mod_rows(l, k):
            return mod_ref[l, k, pl.ds(mod_row, 1), :]
    else:
        pool_in_ref, conv_in_ref = refs[13:15]
        (y_ref, pool_out_ref, conv_out_ref, vs_ref, w_in_b, w_out_b,
         pool_hist, conv_hist, x_carry) = refs[15:]
        tile = 0
        step = pl.program_id(0)

        @pl.when(step == 0)
        def _():
            pool_hist[:, :, 0:1, :] = jnp.zeros((depth, nseg, 1, POOL_W), jnp.float32)
            pool_hist[:, :, 1:, :] = pool_in_ref[...]
            conv_hist[:, :, :CONV_HIST_ROWS - nconv, :] = jnp.zeros(
                (depth, nseg, CONV_HIST_ROWS - nconv, CONV_W), jnp.float32)
            conv_hist[:, :, CONV_HIST_ROWS - nconv:, :] = conv_in_ref[...]
            x_carry[...] = x_ref[...].reshape(R, D_MODEL)

        w_in_b[...] = w_in_ref[...].astype(jnp.bfloat16)
        w_out_b[...] = w_out_ref[...].astype(jnp.bfloat16)

        def mod_rows(l, k):
            m = mod_ref[l, k]
            return jnp.concatenate(
                [jnp.broadcast_to(m[i:i + 1], (L, D_MODEL)) for i in range(nseg)], axis=0)

    lane = lax.broadcasted_iota(jnp.int32, (1, LANES), 1)
    low_half = lane < POOL_GROUP

    inv_w = jnp.concatenate(
        [jnp.full((1, POOL_GROUP), 1.0 / w, jnp.float32) for w in POOL_WINDOWS], axis=-1)
    win = jnp.concatenate(
        [jnp.full((1, POOL_GROUP), float(w), jnp.float32) for w in POOL_WINDOWS], axis=-1)
    pos_top = (tile * L + pos0
               + lax.broadcasted_iota(jnp.int32, (POOL_HIST_ROWS, 1), 0)).astype(jnp.float32)
    cnt_top = jnp.minimum(pos_top + 1.0, win)

    def layer(l, wl, x):
        shift, scale, gate = mod_rows(l, 0), mod_rows(l, 1), mod_rows(l, 2)
        amp = _row(norm_g_ref, l) * (1.0 + scale)
        ms = jnp.mean(x * x, axis=-1, keepdims=True)
        hb = ((x * lax.rsqrt(ms + EPS)) * amp + shift).astype(jnp.bfloat16)
        z = jnp.dot(hb, w_in_b[wl], preferred_element_type=jnp.float32)
        p, zp, gb, gc, hx, zc, u, v, zg = (z[:, _OFFS[i]:_OFFS[i + 1]] for i in range(9))

        e = _ext_rows(pool_hist[l], p)
        s2 = e + pltpu.roll(e, 1, 0)
        s4 = s2 + pltpu.roll(s2, 2, 0)
        s4b = s4[:, LANES:]
        s8 = s4b + pltpu.roll(s4b, 4, 0)
        s16 = s8 + pltpu.roll(s8, 8, 0)
        wsum = jnp.concatenate([jnp.where(low_half, s2[:, :LANES], s4[:, :LANES]),
                                jnp.where(low_half, s8, s16)], axis=-1)
        wsum3 = _unext(wsum, nseg, POOL_HIST_ROWS).reshape(nseg, L, POOL_W)
        p3 = p.reshape(nseg, L, POOL_W)
        d_top = wsum3[:, :POOL_HIST_ROWS] / cnt_top[None] - p3[:, :POOL_HIST_ROWS]
        if L > POOL_HIST_ROWS:
            d_rest = wsum3[:, POOL_HIST_ROWS:] * inv_w[None] - p3[:, POOL_HIST_ROWS:]
            d3 = jnp.concatenate([d_top, d_rest], axis=1)
        else:
            d3 = d_top
        pooled = jnp.dot(d3.reshape(R, POOL_W).astype(jnp.bfloat16), wpool_ref[l],
                         preferred_element_type=jnp.float32) * _row(pscale_ref, l)
        y_pool = pooled * _silu(zp)
        pool_out_ref[l] = p3[:, L - npool:, :].reshape(pool_out_ref.shape[1:])
        if not streamed:
            pool_hist[l] = p3[:, L - POOL_HIST_ROWS:, :]

        q = gc * hx
        qe = _ext_rows(conv_hist[l], q)
        wc = wconv_ref[l]
        conv = wc[2:3] * qe + wc[1:2] * pltpu.roll(qe, 1, 0) + wc[0:1] * pltpu.roll(qe, 2, 0)
        y_conv = gb * _unext(conv, nseg, CONV_HIST_ROWS) * _silu(zc)
        q3 = q.reshape(nseg, L, CONV_W)
        conv_out_ref[l] = q3[:, L - nconv:, :].reshape(conv_out_ref.shape[1:])
        if not streamed:
            conv_hist[l] = q3[:, L - CONV_HIST_ROWS:, :]

        mu = jnp.mean(v, axis=-1, keepdims=True)
        vc = v - mu
        var = jnp.mean(vc * vc, axis=-1, keepdims=True)
        vn = (vc * lax.rsqrt(var + EPS)) * _row(vng_ref, l) + _row(vnb_ref, l)
        if vs_ref is not None:
            vs_ref[l] = vn.reshape(nseg, L, GMLP_W)
        mixed_chunks = []
        for c in range(R // gchunk):
            cols = []
            for j in range(GMLP_W // LANES):
                vcj = vn[c * gchunk:(c + 1) * gchunk, j * LANES:(j + 1) * LANES]
                rhs = jnp.concatenate([jnp.where(low_half, vcj, 0.0), jnp.where(low_half, 0.0, vcj)],
                                      axis=0).astype(jnp.bfloat16)
                cols.append(jnp.dot(glhs_ref[l, j], rhs, preferred_element_type=jnp.float32))
            mixed_chunks.append(jnp.concatenate(cols, axis=-1) + gbias_ref[l])
        mixed = mixed_chunks[0] if len(mixed_chunks) == 1 else jnp.concatenate(mixed_chunks, axis=0)
        y_gmlp = (u * mixed) * _silu(zg)

        ycat = jnp.concatenate([y_pool, y_conv, y_gmlp], axis=-1).astype(jnp.bfloat16)
        rb = min(OUT_ROW_BLOCK, R)
        y = jnp.concatenate(
            [jnp.dot(ycat[r:r + rb], w_out_b[wl], preferred_element_type=jnp.float32)
             for r in range(0, R, rb)], axis=0)
        return x + gate * y

    def final_norm(x):
        ms = jnp.mean(x * x, axis=-1, keepdims=True)
        y_ref[...] = ((x * lax.rsqrt(ms + EPS)) * fin_g_ref[...]).reshape(y_ref.shape)

    if not streamed:
        x = x_ref[...].reshape(R, D_MODEL)
        for l in range(depth):
            x = layer(l, l, x)
        final_norm(x)
    else:
        x = layer(step, 0, x_carry[...])
        x_carry[...] = x

        @pl.when(step == depth - 1)
        def _():
            final_norm(x)


def _const_spec(shape):
    nd = len(shape)
    return pl.BlockSpec(shape, lambda *_: (0,) * nd, pipeline_mode=pl.Buffered(1))


def _layer_spec(shape, buffers=2):
    nd = len(shape)
    return pl.BlockSpec((1,) + tuple(shape[1:]), lambda l: (l,) + (0,) * (nd - 1),
                        pipeline_mode=pl.Buffered(buffers))


def _gating_operands(w_s, b_s, n):
    blk = jnp.arange(n) // CHUNK
    mask = blk[None, :] <= blk[:, None]
    w = jnp.where(mask[None, None], w_s[:, :, :n, :n], 0.0).astype(jnp.bfloat16)
    lhs = jnp.concatenate([w[:, 0::2], w[:, 1::2]], axis=-1)
    bias = jnp.repeat(jnp.swapaxes(b_s[:, :, :n], 1, 2), GMLP_HEAD_DIM, axis=-1)
    return lhs, bias


def kernel(x_prompt, x_sample, state_pool, state_conv, c_prompt, c_sample, norm_g, w_ada, b_ada, w_in, w_pool, pool_scale, w_conv, v_norm_g, v_norm_b, w_s, b_s, w_out, final_norm_g):
    depth = w_in.shape[0]
    nb, seq, d = x_prompt.shape
    ndec, dec_seq, _ = x_sample.shape
    npool, nconv = state_pool.shape[2], state_conv.shape[2]
    assert npool == POOL_HIST_ROWS - 1 and nconv == CONV_K - 1 and d == D_MODEL
    f32, bf16 = jnp.float32, jnp.bfloat16

    mod_p, mod_s = pl.pallas_call(
        _mod_kernel,
        grid=(depth, 3),
        in_specs=[pl.BlockSpec((nb, d), lambda l, k: (0, 0)),
                  pl.BlockSpec((ndec, d), lambda l, k: (0, 0)),
                  pl.BlockSpec((1, d, d), lambda l, k: (l, 0, k)),
                  pl.BlockSpec((depth, d), lambda l, k: (0, k))],
        out_specs=[pl.BlockSpec((1, 1, nb, d), lambda l, k: (l, k, 0, 0)),
                   pl.BlockSpec((1, 1, ndec, d), lambda l, k: (l, k, 0, 0))],
        out_shape=[jax.ShapeDtypeStruct((depth, 3, nb, d), f32),
                   jax.ShapeDtypeStruct((depth, 3, ndec, d), f32)],
        name="adaln_mod",
    )(c_prompt, c_sample, w_ada, b_ada)

    ngroups = len(POOL_WINDOWS)
    wpool_bd = jnp.einsum('ab,lade->ladbe', jnp.eye(ngroups, dtype=f32), w_pool)
    wpool_bd = wpool_bd.reshape(depth, POOL_W, POOL_W).astype(bf16)
    fin_g2 = final_norm_g.reshape(1, d)
    small = (wpool_bd, pool_scale, w_conv, v_norm_g, v_norm_b)

    assert dec_seq >= POOL_HIST_ROWS and dec_seq % (2 * SUBLANES) == 0 and dec_seq <= CHUNK
    glhs_s, gbias_s = _gating_operands(w_s, b_s, dec_seq)
    sample_in = ((x_sample, mod_s, norm_g, w_in) + small + (glhs_s, gbias_s, w_out, fin_g2)
                 + (state_pool, state_conv))
    sample_specs = [_layer_spec(a.shape, 1) if a is w_in else _layer_spec(a.shape) if a is w_out
                    else _const_spec(a.shape) for a in sample_in]
    sample_out = [jax.ShapeDtypeStruct((ndec, dec_seq, d), f32),
                  jax.ShapeDtypeStruct((depth, ndec, npool, POOL_W), f32),
                  jax.ShapeDtypeStruct((depth, ndec, nconv, CONV_W), f32),
                  jax.ShapeDtypeStruct((depth, ndec, dec_seq, GMLP_W), f32),
                  jax.ShapeDtypeStruct(w_in.shape, bf16),
                  jax.ShapeDtypeStruct(w_out.shape, bf16)]
    sample_out_specs = [pl.BlockSpec(s.shape, lambda l, nd=len(s.shape): (0,) * nd)
                        for s in sample_out[:4]]
    sample_out_specs += [_layer_spec(w_in.shape), _layer_spec(w_out.shape)]
    y_sample, pool_s, conv_s, v_s, w_in_b, w_out_b = pl.pallas_call(
        functools.partial(_trunk_kernel, nseg=ndec, rows=dec_seq, gchunk=dec_seq,
                          streamed=True, pos0=PAST_LEN),
        grid=(depth,),
        in_specs=sample_specs,
        out_specs=sample_out_specs,
        out_shape=sample_out,
        scratch_shapes=[pltpu.VMEM((depth, ndec, POOL_HIST_ROWS, POOL_W), f32),
                        pltpu.VMEM((depth, ndec, CONV_HIST_ROWS, CONV_W), f32),
                        pltpu.VMEM((ndec * dec_seq, d), f32)],
        compiler_params=pltpu.CompilerParams(
            dimension_semantics=("arbitrary",), vmem_limit_bytes=VMEM_LIMIT_BYTES),
        name="sample_trunk",
    )(*sample_in)

    tl = min(SEQ_TILE, seq)
    assert seq % tl == 0 and tl % GMLP_CHUNK == 0 and nb % SUBLANES == 0
    glhs_p, gbias_p = _gating_operands(w_s, b_s, GMLP_CHUNK)
    prompt_in = (x_prompt, mod_p, norm_g, w_in_b) + small + (glhs_p, gbias_p, w_out_b, fin_g2)
    prompt_specs = [pl.BlockSpec((1, tl, d), lambda b, t: (b, t, 0)),
                    pl.BlockSpec((depth, 3, SUBLANES, d), lambda b, t: (0, 0, b // SUBLANES, 0))]
    prompt_specs += [_const_spec(a.shape) for a in prompt_in[2:]]
    y_prompt, pool_p, conv_p = pl.pallas_call(
        functools.partial(_trunk_kernel, nseg=1, rows=tl, gchunk=GMLP_CHUNK, streamed=False, pos0=0),
        grid=(nb, seq // tl),
        in_specs=prompt_specs,
        out_specs=[pl.BlockSpec((1, tl, d), lambda b, t: (b, t, 0)),
                   pl.BlockSpec((depth, 1, npool, POOL_W), lambda b, t: (0, b, 0, 0)),
                   pl.BlockSpec((depth, 1, nconv, CONV_W), lambda b, t: (0, b, 0, 0))],
        out_shape=[jax.ShapeDtypeStruct((nb, seq, d), f32),
                   jax.ShapeDtypeStruct((depth, nb, npool, POOL_W), f32),
                   jax.ShapeDtypeStruct((depth, nb, nconv, CONV_W), f32)],
        scratch_shapes=[pltpu.VMEM((depth, 1, POOL_HIST_ROWS, POOL_W), f32),
                        pltpu.VMEM((depth, 1, CONV_HIST_ROWS, CONV_W), f32)],
        compiler_params=pltpu.CompilerParams(
            dimension_semantics=("arbitrary", "arbitrary"), vmem_limit_bytes=VMEM_LIMIT_BYTES),
        name="prompt_trunk",
    )(*prompt_in)

    return y_prompt, y_sample, pool_p, conv_p, pool_s, conv_s, v_s
```

```python
import functools

import numpy as np
import jax
import jax.numpy as jnp
from jax import lax
from jax.experimental import pallas as pl
from jax.experimental.pallas import tpu as pltpu

D_MODEL = 1024
POOL_WINDOWS = (2, 4, 8, 16)
POOL_GROUP = 64
POOL_W = 256
CONV_W = 384
CONV_K = 3
GMLP_W = 384
GMLP_HEADS = 6
GMLP_HEAD_DIM = 64
GMLP_CHUNK = 128
CHUNK = 64
PROJ_W = 3200
PAST_LEN = 1024
EPS = 1e-6

_SPLITS = (256, 256, 384, 384, 384, 384, 384, 384, 384)
_OFFS = tuple(int(v) for v in np.cumsum((0,) + _SPLITS))

SUBLANES = 8
LANES = 128
POOL_HIST_ROWS = 2 * SUBLANES
CONV_HIST_ROWS = SUBLANES
SEQ_TILE = 1024
GMLP_CHUNKS_PER_DOT = 1
OUT_ROW_BLOCK = 256
VMEM_LIMIT_BYTES = 56 * 1024 * 1024


def _silu(z):
    return z * (0.5 * jnp.tanh(0.5 * z) + 0.5)


def _mod_kernel(cp_ref, cs_ref, w_ref, b_ref, op_ref, os_ref):
    nb = cp_ref.shape[0]
    layer = pl.program_id(0)
    c = jnp.concatenate([cp_ref[...], cs_ref[...]], axis=0)
    mod = jnp.dot(_silu(c), w_ref[0], preferred_element_type=jnp.float32) + b_ref[pl.ds(layer, 1), :]
    op_ref[0, 0] = mod[:nb]
    os_ref[0, 0] = mod[nb:]


def _ext_rows(hist3, val2):
    nseg, h, c = hist3.shape
    l = val2.shape[0] // nseg
    return jnp.concatenate([hist3, val2.reshape(nseg, l, c)], axis=1).reshape(nseg * (h + l), c)


def _unext(e2, nseg, h):
    l = e2.shape[0] // nseg - h
    return e2.reshape(nseg, h + l, e2.shape[-1])[:, h:, :].reshape(nseg * l, e2.shape[-1])


def _row(ref, l):
    return ref[pl.ds(l, 1), :]


def _trunk_kernel(*refs, nseg, rows, gchunk, streamed, pos0):
    (x_ref, mod_ref, norm_g_ref, w_in_ref, wpool_ref, pscale_ref, wconv_ref, vng_ref, vnb_ref,
     glhs_ref, gbias_ref, w_out_ref, fin_g_ref) = refs[:13]
    L = rows
    R = nseg * L
    depth = norm_g_ref.shape[0]
    npool = POOL_HIST_ROWS - 1
    nconv = CONV_K - 1
    if not streamed:
        y_ref, pool_out_ref, conv_out_ref, pool_hist, conv_hist = refs[13:]
        vs_ref = None
        w_in_b, w_out_b = w_in_ref, w_out_ref
        tile = pl.program_id(1)
        mod_row = pl.program_id(0) % SUBLANES

        @pl.when(tile == 0)
        def _():
            pool_hist[...] = jnp.zeros_like(pool_hist)
            conv_hist[...] = jnp.zeros_like(conv_hist)

        def mod_rows(l, k):
            return mod_ref[l, k, pl.ds(mod_row, 1), :]
    else:
        pool_in_ref, conv_in_ref = refs[13:15]
        (y_ref, pool_out_ref, conv_out_ref, vs_ref, w_in_b, w_out_b,
         pool_hist, conv_hist, x_carry) = refs[15:]
        tile = 0
        step = pl.program_id(0)

        @pl.when(step == 0)
        def _():
            pool_hist[:, :, 0:1, :] = jnp.zeros((depth, nseg, 1, POOL_W), jnp.float32)
            pool_hist[:, :, 1:, :] = pool_in_ref[...]
            conv_hist[:, :, :CONV_HIST_ROWS - nconv, :] = jnp.zeros(
                (depth, nseg, CONV_HIST_ROWS - nconv, CONV_W), jnp.float32)
            conv_hist[:, :, CONV_HIST_ROWS - nconv:, :] = conv_in_ref[...]
            x_carry[...] = x_ref[...].reshape(R, D_MODEL)

        w_in_b[...] = w_in_ref[...].astype(jnp.bfloat16)
        w_out_b[...] = w_out_ref[...].astype(jnp.bfloat16)

        def mod_rows(l, k):
            m = mod_ref[l, k]
            return jnp.concatenate(
                [jnp.broadcast_to(m[i:i + 1], (L, D_MODEL)) for i in range(nseg)], axis=0)

    lane = lax.broadcasted_iota(jnp.int32, (1, LANES), 1)
    low_half = lane < POOL_GROUP

    inv_w = jnp.concatenate(
        [jnp.full((1, POOL_GROUP), 1.0 / w, jnp.float32) for w in POOL_WINDOWS], axis=-1)
    win = jnp.concatenate(
        [jnp.full((1, POOL_GROUP), float(w), jnp.float32) for w in POOL_WINDOWS], axis=-1)
    pos_top = (tile * L + pos0
               + lax.broadcasted_iota(jnp.int32, (POOL_HIST_ROWS, 1), 0)).astype(jnp.float32)
    cnt_top = jnp.minimum(pos_top + 1.0, win)

    def layer(l, wl, x):
        shift, scale, gate = mod_rows(l, 0), mod_rows(l, 1), mod_rows(l, 2)
        amp = _row(norm_g_ref, l) * (1.0 + scale)
        ms = jnp.mean(x * x, axis=-1, keepdims=True)
        hb = ((x * lax.rsqrt(ms + EPS)) * amp + shift).astype(jnp.bfloat16)
        z = jnp.dot(hb, w_in_b[wl], preferred_element_type=jnp.float32)
        p, zp, gb, gc, hx, zc, u, v, zg = (z[:, _OFFS[i]:_OFFS[i + 1]] for i in range(9))

        e = _ext_rows(pool_hist[l], p)
        s2 = e + pltpu.roll(e, 1, 0)
        s4 = s2 + pltpu.roll(s2, 2, 0)
        s4b = s4[:, LANES:]
        s8 = s4b + pltpu.roll(s4b, 4, 0)
        s16 = s8 + pltpu.roll(s8, 8, 0)
        wsum = jnp.concatenate([jnp.where(low_half, s2[:, :LANES], s4[:, :LANES]),
                                jnp.where(low_half, s8, s16)], axis=-1)
        wsum3 = _unext(wsum, nseg, POOL_HIST_ROWS).reshape(nseg, L, POOL_W)
        p3 = p.reshape(nseg, L, POOL_W)
        d_top = wsum3[:, :POOL_HIST_ROWS] / cnt_top[None] - p3[:, :POOL_HIST_ROWS]
        if L > POOL_HIST_ROWS:
            d_rest = wsum3[:, POOL_HIST_ROWS:] * inv_w[None] - p3[:, POOL_HIST_ROWS:]
            d3 = jnp.concatenate([d_top, d_rest], axis=1)
        else:
            d3 = d_top
        pooled = jnp.dot(d3.reshape(R, POOL_W).astype(jnp.bfloat16), wpool_ref[l],
                         preferred_element_type=jnp.float32) * _row(pscale_ref, l)
        y_pool = pooled * _silu(zp)
        pool_out_ref[l] = p3[:, L - npool:, :].reshape(pool_out_ref.shape[1:])
        if not streamed:
            pool_hist[l] = p3[:, L - POOL_HIST_ROWS:, :]

        q = gc * hx
        qe = _ext_rows(conv_hist[l], q)
        wc = wconv_ref[l]
        conv = wc[2:3] * qe + wc[1:2] * pltpu.roll(qe, 1, 0) + wc[0:1] * pltpu.roll(qe, 2, 0)
        y_conv = gb * _unext(conv, nseg, CONV_HIST_ROWS) * _silu(zc)
        q3 = q.reshape(nseg, L, CONV_W)
        conv_out_ref[l] = q3[:, L - nconv:, :].reshape(conv_out_ref.shape[1:])
        if not streamed:
            conv_hist[l] = q3[:, L - CONV_HIST_ROWS:, :]

        mu = jnp.mean(v, axis=-1, keepdims=True)
        vc = v - mu
        var = jnp.mean(vc * vc, axis=-1, keepdims=True)
        vn = (vc * lax.rsqrt(var + EPS)) * _row(vng_ref, l) + _row(vnb_ref, l)
        if vs_ref is not None:
            vs_ref[l] = vn.reshape(nseg, L, GMLP_W)
        nchunk = R // gchunk
        cpd = GMLP_CHUNKS_PER_DOT if nchunk % GMLP_CHUNKS_PER_DOT == 0 else 1
        vnb16 = vn.astype(jnp.bfloat16)
        zero16 = jnp.zeros((), jnp.bfloat16)
        mixed_chunks = [[] for _ in range(nchunk)]
        for c in range(0, nchunk, cpd):
            for j in range(GMLP_W // LANES):
                rhs = []
                for cc in range(c, c + cpd):
                    vcj = vnb16[cc * gchunk:(cc + 1) * gchunk, j * LANES:(j + 1) * LANES]
                    rhs.append(jnp.concatenate(
                        [jnp.where(low_half, vcj, zero16), jnp.where(low_half, zero16, vcj)], axis=0))
                rhs = rhs[0] if cpd == 1 else jnp.concatenate(rhs, axis=-1)
                m = jnp.dot(glhs_ref[l, j], rhs, preferred_element_type=jnp.float32)
                for k in range(cpd):
                    mixed_chunks[c + k].append(m[:, k * LANES:(k + 1) * LANES])
        mixed = jnp.concatenate(
            [jnp.concatenate(cols, axis=-1) + gbias_ref[l] for cols in mixed_chunks], axis=0)
        y_gmlp = (u * mixed) * _silu(zg)

        ycat = jnp.concatenate([y_pool, y_conv, y_gmlp], axis=-1).astype(jnp.bfloat16)
        rb = min(OUT_ROW_BLOCK, R)
        y = jnp.concatenate(
            [jnp.dot(ycat[r:r + rb], w_out_b[wl], preferred_element_type=jnp.float32)
             for r in range(0, R, rb)], axis=0)
        return x + gate * y

    def final_norm(x):
        ms = jnp.mean(x * x, axis=-1, keepdims=True)
        y_ref[...] = ((x * lax.rsqrt(ms + EPS)) * fin_g_ref[...]).reshape(y_ref.shape)

    if not streamed:
        x = x_ref[...].reshape(R, D_MODEL)
        for l in range(depth):
            x = layer(l, l, x)
        final_norm(x)
    else:
        x = layer(step, 0, x_carry[...])
        x_carry[...] = x

        @pl.when(step == depth - 1)
        def _():
            final_norm(x)


def _const_spec(shape):
    nd = len(shape)
    return pl.BlockSpec(shape, lambda *_: (0,) * nd, pipeline_mode=pl.Buffered(1))


def _layer_spec(shape, buffers=2):
    nd = len(shape)
    return pl.BlockSpec((1,) + tuple(shape[1:]), lambda l: (l,) + (0,) * (nd - 1),
                        pipeline_mode=pl.Buffered(buffers))


def _gating_operands(w_s, b_s, n):
    blk = jnp.arange(n) // CHUNK
    mask = blk[None, :] <= blk[:, None]
    w = jnp.where(mask[None, None], w_s[:, :, :n, :n], 0.0).astype(jnp.bfloat16)
    lhs = jnp.concatenate([w[:, 0::2], w[:, 1::2]], axis=-1)
    bias = jnp.repeat(jnp.swapaxes(b_s[:, :, :n], 1, 2), GMLP_HEAD_DIM, axis=-1)
    return lhs, bias


def kernel(x_prompt, x_sample, state_pool, state_conv, c_prompt, c_sample, norm_g, w_ada, b_ada, w_in, w_pool, pool_scale, w_conv, v_norm_g, v_norm_b, w_s, b_s, w_out, final_norm_g):
    depth = w_in.shape[0]
    nb, seq, d = x_prompt.shape
    ndec, dec_seq, _ = x_sample.shape
    npool, nconv = state_pool.shape[2], state_conv.shape[2]
    assert npool == POOL_HIST_ROWS - 1 and nconv == CONV_K - 1 and d == D_MODEL
    f32, bf16 = jnp.float32, jnp.bfloat16

    mod_p, mod_s = pl.pallas_call(
        _mod_kernel,
        grid=(depth, 3),
        in_specs=[pl.BlockSpec((nb, d), lambda l, k: (0, 0)),
                  pl.BlockSpec((ndec, d), lambda l, k: (0, 0)),
                  pl.BlockSpec((1, d, d), lambda l, k: (l, 0, k)),
                  pl.BlockSpec((depth, d), lambda l, k: (0, k))],
        out_specs=[pl.BlockSpec((1, 1, nb, d), lambda l, k: (l, k, 0, 0)),
                   pl.BlockSpec((1, 1, ndec, d), lambda l, k: (l, k, 0, 0))],
        out_shape=[jax.ShapeDtypeStruct((depth, 3, nb, d), f32),
                   jax.ShapeDtypeStruct((depth, 3, ndec, d), f32)],
        name="adaln_mod",
    )(c_prompt, c_sample, w_ada, b_ada)

    ngroups = len(POOL_WINDOWS)
    wpool_bd = jnp.einsum('ab,lade->ladbe', jnp.eye(ngroups, dtype=f32), w_pool)
    wpool_bd = wpool_bd.reshape(depth, POOL_W, POOL_W).astype(bf16)
    fin_g2 = final_norm_g.reshape(1, d)
    small = (wpool_bd, pool_scale, w_conv, v_norm_g, v_norm_b)

    assert dec_seq >= POOL_HIST_ROWS and dec_seq % (2 * SUBLANES) == 0 and dec_seq <= CHUNK
    glhs_s, gbias_s = _gating_operands(w_s, b_s, dec_seq)
    sample_in = ((x_sample, mod_s, norm_g, w_in) + small + (glhs_s, gbias_s, w_out, fin_g2)
                 + (state_pool, state_conv))
    sample_specs = [_layer_spec(a.shape, 1) if a is w_in else _layer_spec(a.shape) if a is w_out
                    else _const_spec(a.shape) for a in sample_in]
    sample_out = [jax.ShapeDtypeStruct((ndec, dec_seq, d), f32),
                  jax.ShapeDtypeStruct((depth, ndec, npool, POOL_W), f32),
                  jax.ShapeDtypeStruct((depth, ndec, nconv, CONV_W), f32),
                  jax.ShapeDtypeStruct((depth, ndec, dec_seq, GMLP_W), f32),
                  jax.ShapeDtypeStruct(w_in.shape, bf16),
                  jax.ShapeDtypeStruct(w_out.shape, bf16)]
    sample_out_specs = [pl.BlockSpec(s.shape, lambda l, nd=len(s.shape): (0,) * nd)
                        for s in sample_out[:4]]
    sample_out_specs += [_layer_spec(w_in.shape), _layer_spec(w_out.shape)]
    y_sample, pool_s, conv_s, v_s, w_in_b, w_out_b = pl.pallas_call(
        functools.partial(_trunk_kernel, nseg=ndec, rows=dec_seq, gchunk=dec_seq,
                          streamed=True, pos0=PAST_LEN),
        grid=(depth,),
        in_specs=sample_specs,
        out_specs=sample_out_specs,
        out_shape=sample_out,
        scratch_shapes=[pltpu.VMEM((depth, ndec, POOL_HIST_ROWS, POOL_W), f32),
                        pltpu.VMEM((depth, ndec, CONV_HIST_ROWS, CONV_W), f32),
                        pltpu.VMEM((ndec * dec_seq, d), f32)],
        compiler_params=pltpu.CompilerParams(
            dimension_semantics=("arbitrary",), vmem_limit_bytes=VMEM_LIMIT_BYTES),
        name="sample_trunk",
    )(*sample_in)

    tl = min(SEQ_TILE, seq)
    assert seq % tl == 0 and tl % GMLP_CHUNK == 0 and nb % SUBLANES == 0
    glhs_p, gbias_p = _gating_operands(w_s, b_s, GMLP_CHUNK)
    prompt_in = (x_prompt, mod_p, norm_g, w_in_b) + small + (glhs_p, gbias_p, w_out_b, fin_g2)
    prompt_specs = [pl.BlockSpec((1, tl, d), lambda b, t: (b, t, 0)),
                    pl.BlockSpec((depth, 3, SUBLANES, d), lambda b, t: (0, 0, b // SUBLANES, 0))]
    prompt_specs += [_const_spec(a.shape) for a in prompt_in[2:]]
    y_prompt, pool_p, conv_p = pl.pallas_call(
        functools.partial(_trunk_kernel, nseg=1, rows=tl, gchunk=GMLP_CHUNK, streamed=False, pos0=0),
        grid=(nb, seq // tl),
        in_specs=prompt_specs,
        out_specs=[pl.BlockSpec((1, tl, d), lambda b, t: (b, t, 0)),
                   pl.BlockSpec((depth, 1, npool, POOL_W), lambda b, t: (0, b, 0, 0)),
                   pl.BlockSpec((depth, 1, nconv, CONV_W), lambda b, t: (0, b, 0, 0))],
        out_shape=[jax.ShapeDtypeStruct((nb, seq, d), f32),
                   jax.ShapeDtypeStruct((depth, nb, npool, POOL_W), f32),
                   jax.ShapeDtypeStruct((depth, nb, nconv, CONV_W), f32)],
        scratch_shapes=[pltpu.VMEM((depth, 1, POOL_HIST_ROWS, POOL_W), f32),
                        pltpu.VMEM((depth, 1, CONV_HIST_ROWS, CONV_W), f32)],
        compiler_params=pltpu.CompilerParams(
            dimension_semantics=("arbitrary", "arbitrary"), vmem_limit_bytes=VMEM_LIMIT_BYTES),
        name="prompt_trunk",
    )(*prompt_in)

    return y_prompt, y_sample, pool_p, conv_p, pool_s, conv_s, v_s
```

```python
import functools

import numpy as np
import jax
import jax.numpy as jnp
from jax import lax
from jax.experimental import pallas as pl
from jax.experimental.pallas import tpu as pltpu

D_MODEL = 1024
POOL_WINDOWS = (2, 4, 8, 16)
POOL_GROUP = 64
POOL_W = 256
CONV_W = 384
CONV_K = 3
GMLP_W = 384
GMLP_HEADS = 6
GMLP_HEAD_DIM = 64
GMLP_CHUNK = 128
CHUNK = 64
PROJ_W = 3200
PAST_LEN = 1024
EPS = 1e-6

_SPLITS = (256, 256, 384, 384, 384, 384, 384, 384, 384)
_OFFS = tuple(int(v) for v in np.cumsum((0,) + _SPLITS))

SUBLANES = 8
LANES = 128
POOL_HIST_ROWS = 2 * SUBLANES
CONV_HIST_ROWS = SUBLANES
SEQ_TILE = 1024
OUT_ROW_BLOCK = 256
VMEM_LIMIT_BYTES = 56 * 1024 * 1024


def _silu(z):
    return z * (0.5 * jnp.tanh(0.5 * z) + 0.5)


def _mod_kernel(cp_ref, cs_ref, w_ref, b_ref, op_ref, os_ref):
    nb, d = cp_ref.shape
    layer = pl.program_id(0)
    c = jnp.concatenate([cp_ref[...], cs_ref[...]], axis=0)
    mod = jnp.dot(_silu(c), w_ref[0], preferred_element_type=jnp.float32) + b_ref[pl.ds(layer, 1), :]
    for k in range(3):
        op_ref[0, k] = mod[:nb, k * d:(k + 1) * d]
        os_ref[0, k] = mod[nb:, k * d:(k + 1) * d]


def _ext_rows(hist3, val2):
    nseg, h, c = hist3.shape
    l = val2.shape[0] // nseg
    return jnp.concatenate([hist3, val2.reshape(nseg, l, c)], axis=1).reshape(nseg * (h + l), c)


def _unext(e2, nseg, h):
    l = e2.shape[0] // nseg - h
    return e2.reshape(nseg, h + l, e2.shape[-1])[:, h:, :].reshape(nseg * l, e2.shape[-1])


def _row(ref, l):
    return ref[pl.ds(l, 1), :]


def _trunk_kernel(*refs, nseg, rows, gchunk, streamed, pos0):
    (x_ref, mod_ref, norm_g_ref, w_in_ref, wpool_ref, pscale_ref, wconv_ref, vng_ref, vnb_ref,
     glhs_ref, gbias_ref, w_out_ref, fin_g_ref) = refs[:13]
    L = rows
    R = nseg * L
    depth = norm_g_ref.shape[0]
    npool = POOL_HIST_ROWS - 1
    nconv = CONV_K - 1
    if not streamed:
        y_ref, pool_out_ref, conv_out_ref, pool_hist, conv_hist = refs[13:]
        vs_ref = None
        w_in_b, w_out_b = w_in_ref, w_out_ref
        tile = pl.program_id(1)
        mod_row = pl.program_id(0) % SUBLANES

        @pl.when(tile == 0)
        def _():
            pool_hist[...] = jnp.zeros_like(pool_hist)
            conv_hist[...] = jnp.zeros_like(conv_hist)

        def mod_rows(l, k):
            return mod_ref[l, k, pl.ds(mod_row, 1), :]
    else:
        pool_in_ref, conv_in_ref = refs[13:15]
        (y_ref, pool_out_ref, conv_out_ref, vs_ref, w_in_b, w_out_b,
         pool_hist, conv_hist, x_carry) = refs[15:]
        tile = 0
        step = pl.program_id(0)

        @pl.when(step == 0)
        def _():
            pool_hist[:, :, 0:1, :] = jnp.zeros((depth, nseg, 1, POOL_W), jnp.float32)
            pool_hist[:, :, 1:, :] = pool_in_ref[...]
            conv_hist[:, :, :CONV_HIST_ROWS - nconv, :] = jnp.zeros(
                (depth, nseg, CONV_HIST_ROWS - nconv, CONV_W), jnp.float32)
            conv_hist[:, :, CONV_HIST_ROWS - nconv:, :] = conv_in_ref[...]
            x_carry[...] = x_ref[...].reshape(R, D_MODEL)

        w_in_b[...] = w_in_ref[...].astype(jnp.bfloat16)
        w_out_b[...] = w_out_ref[...].astype(jnp.bfloat16)

        def mod_rows(l, k):
            m = mod_ref[l, k]
            return jnp.concatenate(
                [jnp.broadcast_to(m[i:i + 1], (L, D_MODEL)) for i in range(nseg)], axis=0)

    lane = lax.broadcasted_iota(jnp.int32, (1, LANES), 1)
    low_half = lane < POOL_GROUP

    inv_w = jnp.concatenate(
        [jnp.full((1, POOL_GROUP), 1.0 / w, jnp.float32) for w in POOL_WINDOWS], axis=-1)
    win = jnp.concatenate(
        [jnp.full((1, POOL_GROUP), float(w), jnp.float32) for w in POOL_WINDOWS], axis=-1)
    pos_top = (tile * L + pos0
               + lax.broadcasted_iota(jnp.int32, (POOL_HIST_ROWS, 1), 0)).astype(jnp.float32)
    cnt_top = jnp.minimum(pos_top + 1.0, win)

    def layer(l, wl, x):
        shift, scale, gate = mod_rows(l, 0), mod_rows(l, 1), mod_rows(l, 2)
        amp = _row(norm_g_ref, l) * (1.0 + scale)
        ms = jnp.mean(x * x, axis=-1, keepdims=True)
        hb = ((x * lax.rsqrt(ms + EPS)) * amp + shift).astype(jnp.bfloat16)
        z = jnp.dot(hb, w_in_b[wl], preferred_element_type=jnp.float32)
        p, zp, gb, gc, hx, zc, u, v, zg = (z[:, _OFFS[i]:_OFFS[i + 1]] for i in range(9))

        e = _ext_rows(pool_hist[l], p)
        s2 = e + pltpu.roll(e, 1, 0)
        s4 = s2 + pltpu.roll(s2, 2, 0)
        s4b = s4[:, LANES:]
        s8 = s4b + pltpu.roll(s4b, 4, 0)
        s16 = s8 + pltpu.roll(s8, 8, 0)
        wsum = jnp.concatenate([jnp.where(low_half, s2[:, :LANES], s4[:, :LANES]),
                                jnp.where(low_half, s8, s16)], axis=-1)
        wsum3 = _unext(wsum, nseg, POOL_HIST_ROWS).reshape(nseg, L, POOL_W)
        p3 = p.reshape(nseg, L, POOL_W)
        d_top = wsum3[:, :POOL_HIST_ROWS] / cnt_top[None] - p3[:, :POOL_HIST_ROWS]
        if L > POOL_HIST_ROWS:
            d_rest = wsum3[:, POOL_HIST_ROWS:] * inv_w[None] - p3[:, POOL_HIST_ROWS:]
            d3 = jnp.concatenate([d_top, d_rest], axis=1)
        else:
            d3 = d_top
        pooled = jnp.dot(d3.reshape(R, POOL_W).astype(jnp.bfloat16), wpool_ref[l],
                         preferred_element_type=jnp.float32) * _row(pscale_ref, l)
        y_pool = pooled * _silu(zp)
        pool_out_ref[l] = p3[:, L - npool:, :].reshape(pool_out_ref.shape[1:])
        if not streamed:
            pool_hist[l] = p3[:, L - POOL_HIST_ROWS:, :]

        q = gc * hx
        qe = _ext_rows(conv_hist[l], q)
        wc = wconv_ref[l]
        conv = wc[2:3] * qe + wc[1:2] * pltpu.roll(qe, 1, 0) + wc[0:1] * pltpu.roll(qe, 2, 0)
        y_conv = gb * _unext(conv, nseg, CONV_HIST_ROWS) * _silu(zc)
        q3 = q.reshape(nseg, L, CONV_W)
        conv_out_ref[l] = q3[:, L - nconv:, :].reshape(conv_out_ref.shape[1:])
        if not streamed:
            conv_hist[l] = q3[:, L - CONV_HIST_ROWS:, :]

        mu = jnp.mean(v, axis=-1, keepdims=True)
        vc = v - mu
        var = jnp.mean(vc * vc, axis=-1, keepdims=True)
        vn = (vc * lax.rsqrt(var + EPS)) * _row(vng_ref, l) + _row(vnb_ref, l)
        if vs_ref is not None:
            vs_ref[l] = vn.reshape(nseg, L, GMLP_W)
        vnb = vn.astype(jnp.bfloat16)
        zero = jnp.zeros((), jnp.bfloat16)
        mixed_chunks = []
        for c in range(R // gchunk):
            cols = []
            for j in range(GMLP_W // LANES):
                vcj = vnb[c * gchunk:(c + 1) * gchunk, j * LANES:(j + 1) * LANES]
                rhs = jnp.concatenate([jnp.where(low_half, vcj, zero), jnp.where(low_half, zero, vcj)],
                                      axis=0)
                cols.append(jnp.dot(glhs_ref[l, j], rhs, preferred_element_type=jnp.float32))
            mixed_chunks.append(jnp.concatenate(cols, axis=-1) + gbias_ref[l])
        mixed = mixed_chunks[0] if len(mixed_chunks) == 1 else jnp.concatenate(mixed_chunks, axis=0)
        y_gmlp = (u * mixed) * _silu(zg)

        ycat = jnp.concatenate([y_pool, y_conv, y_gmlp], axis=-1).astype(jnp.bfloat16)
        rb = min(OUT_ROW_BLOCK, R)
        y = jnp.concatenate(
            [jnp.dot(ycat[r:r + rb], w_out_b[wl], preferred_element_type=jnp.float32)
             for r in range(0, R, rb)], axis=0)
        return x + gate * y

    def final_norm(x):
        ms = jnp.mean(x * x, axis=-1, keepdims=True)
        y_ref[...] = ((x * lax.rsqrt(ms + EPS)) * fin_g_ref[...]).reshape(y_ref.shape)

    if not streamed:
        x = x_ref[...].reshape(R, D_MODEL)
        for l in range(depth):
            x = layer(l, l, x)
        final_norm(x)
    else:
        x = layer(step, 0, x_carry[...])
        x_carry[...] = x

        @pl.when(step == depth - 1)
        def _():
            final_norm(x)


def _const_spec(shape):
    nd = len(shape)
    return pl.BlockSpec(shape, lambda *_: (0,) * nd, pipeline_mode=pl.Buffered(1))


def _layer_spec(shape, buffers=2):
    nd = len(shape)
    return pl.BlockSpec((1,) + tuple(shape[1:]), lambda l: (l,) + (0,) * (nd - 1),
                        pipeline_mode=pl.Buffered(buffers))


def _gating_operands(w_s, b_s, n):
    blk = jnp.arange(n) // CHUNK
    mask = blk[None, :] <= blk[:, None]
    w = jnp.where(mask[None, None], w_s[:, :, :n, :n], 0.0).astype(jnp.bfloat16)
    lhs = jnp.concatenate([w[:, 0::2], w[:, 1::2]], axis=-1)
    bias = jnp.repeat(jnp.swapaxes(b_s[:, :, :n], 1, 2), GMLP_HEAD_DIM, axis=-1)
    return lhs, bias


def kernel(x_prompt, x_sample, state_pool, state_conv, c_prompt, c_sample, norm_g, w_ada, b_ada, w_in, w_pool, pool_scale, w_conv, v_norm_g, v_norm_b, w_s, b_s, w_out, final_norm_g):
    depth = w_in.shape[0]
    nb, seq, d = x_prompt.shape
    ndec, dec_seq, _ = x_sample.shape
    npool, nconv = state_pool.shape[2], state_conv.shape[2]
    assert npool == POOL_HIST_ROWS - 1 and nconv == CONV_K - 1 and d == D_MODEL
    f32, bf16 = jnp.float32, jnp.bfloat16

    mod_p, mod_s = pl.pallas_call(
        _mod_kernel,
        grid=(depth,),
        in_specs=[pl.BlockSpec((nb, d), lambda l: (0, 0)),
                  pl.BlockSpec((ndec, d), lambda l: (0, 0)),
                  pl.BlockSpec((1, d, 3 * d), lambda l: (l, 0, 0)),
                  pl.BlockSpec((depth, 3 * d), lambda l: (0, 0))],
        out_specs=[pl.BlockSpec((1, 3, nb, d), lambda l: (l, 0, 0, 0)),
                   pl.BlockSpec((1, 3, ndec, d), lambda l: (l, 0, 0, 0))],
        out_shape=[jax.ShapeDtypeStruct((depth, 3, nb, d), f32),
                   jax.ShapeDtypeStruct((depth, 3, ndec, d), f32)],
        name="adaln_mod",
    )(c_prompt, c_sample, w_ada, b_ada)

    ngroups = len(POOL_WINDOWS)
    wpool_bd = jnp.einsum('ab,lade->ladbe', jnp.eye(ngroups, dtype=f32), w_pool)
    wpool_bd = wpool_bd.reshape(depth, POOL_W, POOL_W).astype(bf16)
    fin_g2 = final_norm_g.reshape(1, d)
    small = (wpool_bd, pool_scale, w_conv, v_norm_g, v_norm_b)

    assert dec_seq >= POOL_HIST_ROWS and dec_seq % (2 * SUBLANES) == 0 and dec_seq <= CHUNK
    glhs_s, gbias_s = _gating_operands(w_s, b_s, dec_seq)
    sample_in = ((x_sample, mod_s, norm_g, w_in) + small + (glhs_s, gbias_s, w_out, fin_g2)
                 + (state_pool, state_conv))
    sample_specs = [_layer_spec(a.shape) if a is w_in or a is w_out
                    else _const_spec(a.shape) for a in sample_in]
    sample_out = [jax.ShapeDtypeStruct((ndec, dec_seq, d), f32),
                  jax.ShapeDtypeStruct((depth, ndec, npool, POOL_W), f32),
                  jax.ShapeDtypeStruct((depth, ndec, nconv, CONV_W), f32),
                  jax.ShapeDtypeStruct((depth, ndec, dec_seq, GMLP_W), f32),
                  jax.ShapeDtypeStruct(w_in.shape, bf16),
                  jax.ShapeDtypeStruct(w_out.shape, bf16)]
    sample_out_specs = [pl.BlockSpec(s.shape, lambda l, nd=len(s.shape): (0,) * nd)
                        for s in sample_out[:4]]
    sample_out_specs += [_layer_spec(w_in.shape, 1), _layer_spec(w_out.shape, 1)]
    y_sample, pool_s, conv_s, v_s, w_in_b, w_out_b = pl.pallas_call(
        functools.partial(_trunk_kernel, nseg=ndec, rows=dec_seq, gchunk=dec_seq,
                          streamed=True, pos0=PAST_LEN),
        grid=(depth,),
        in_specs=sample_specs,
        out_specs=sample_out_specs,
        out_shape=sample_out,
        scratch_shapes=[pltpu.VMEM((depth, ndec, POOL_HIST_ROWS, POOL_W), f32),
                        pltpu.VMEM((depth, ndec, CONV_HIST_ROWS, CONV_W), f32),
                        pltpu.VMEM((ndec * dec_seq, d), f32)],
        compiler_params=pltpu.CompilerParams(
            dimension_semantics=("arbitrary",), vmem_limit_bytes=VMEM_LIMIT_BYTES),
        name="sample_trunk",
    )(*sample_in)

    tl = min(SEQ_TILE, seq)
    assert seq % tl == 0 and tl % GMLP_CHUNK == 0 and nb % SUBLANES == 0
    glhs_p, gbias_p = _gating_operands(w_s, b_s, GMLP_CHUNK)
    prompt_in = (x_prompt, mod_p, norm_g, w_in_b) + small + (glhs_p, gbias_p, w_out_b, fin_g2)
    prompt_specs = [pl.BlockSpec((1, tl, d), lambda b, t: (b, t, 0)),
                    pl.BlockSpec((depth, 3, SUBLANES, d), lambda b, t: (0, 0, b // SUBLANES, 0))]
    prompt_specs += [_const_spec(a.shape) for a in prompt_in[2:]]
    y_prompt, pool_p, conv_p = pl.pallas_call(
        functools.partial(_trunk_kernel, nseg=1, rows=tl, gchunk=GMLP_CHUNK, streamed=False, pos0=0),
        grid=(nb, seq // tl),
        in_specs=prompt_specs,
        out_specs=[pl.BlockSpec((1, tl, d), lambda b, t: (b, t, 0)),
                   pl.BlockSpec((depth, 1, npool, POOL_W), lambda b, t: (0, b, 0, 0)),
                   pl.BlockSpec((depth, 1, nconv, CONV_W), lambda b, t: (0, b, 0, 0))],
        out_shape=[jax.ShapeDtypeStruct((nb, seq, d), f32),
                   jax.ShapeDtypeStruct((depth, nb, npool, POOL_W), f32),
                   jax.ShapeDtypeStruct((depth, nb, nconv, CONV_W), f32)],
        scratch_shapes=[pltpu.VMEM((depth, 1, POOL_HIST_ROWS, POOL_W), f32),
                        pltpu.VMEM((depth, 1, CONV_HIST_ROWS, CONV_W), f32)],
        compiler_params=pltpu.CompilerParams(
            dimension_semantics=("arbitrary", "arbitrary"), vmem_limit_bytes=VMEM_LIMIT_BYTES),
        name="prompt_trunk",
    )(*prompt_in)

    return y_prompt, y_sample, pool_p, conv_p, pool_s, conv_s, v_s
```

```python
import functools

import numpy as np
import jax
import jax.numpy as jnp
from jax import lax
from jax.experimental import pallas as pl
from jax.experimental.pallas import tpu as pltpu

D_MODEL = 1024
POOL_WINDOWS = (2, 4, 8, 16)
POOL_GROUP = 64
POOL_W = 256
CONV_W = 384
CONV_K = 3
GMLP_W = 384
GMLP_HEAD_DIM = 64
GMLP_CHUNK = 128
CHUNK = 64
PAST_LEN = 1024
EPS = 1e-6

_SPLITS = (256, 256, 384, 384, 384, 384, 384, 384, 384)
_OFFS = tuple(int(v) for v in np.cumsum((0,) + _SPLITS))

SUBLANES = 8
LANES = 128
POOL_HIST_ROWS = 2 * SUBLANES
CONV_HIST_ROWS = SUBLANES
SEQ_TILE = 1024
OUT_ROW_BLOCK = 256
VMEM_LIMIT_BYTES = 56 * 1024 * 1024


def _silu(z):
    return z * (0.5 * jnp.tanh(0.5 * z) + 0.5)


def _mod_kernel(cp_ref, cs_ref, w_ref, b_ref, op_ref, os_ref):
    nb, d = cp_ref.shape
    layer = pl.program_id(0)
    c = jnp.concatenate([cp_ref[...], cs_ref[...]], axis=0)
    mod = jnp.dot(_silu(c), w_ref[0], preferred_element_type=jnp.float32) + b_ref[pl.ds(layer, 1), :]
    for k in range(3):
        op_ref[0, k] = mod[:nb, k * d:(k + 1) * d]
        os_ref[0, k] = mod[nb:, k * d:(k + 1) * d]


def _ext_rows(hist3, val2):
    nseg, h, c = hist3.shape
    l = val2.shape[0] // nseg
    return jnp.concatenate([hist3, val2.reshape(nseg, l, c)], axis=1).reshape(nseg * (h + l), c)


def _unext(e2, nseg, h):
    l = e2.shape[0] // nseg - h
    return e2.reshape(nseg, h + l, e2.shape[-1])[:, h:, :].reshape(nseg * l, e2.shape[-1])


def _row(ref, l):
    return ref[pl.ds(l, 1), :]


def _trunk_kernel(*refs, nseg, rows, gchunk, streamed, pos0):
    (x_ref, mod_ref, norm_g_ref, w_in_ref, wpool_ref, pscale_ref, wconv_ref, vng_ref, vnb_ref,
     glhs_ref, gbias_ref, w_out_ref, fin_g_ref) = refs[:13]
    L = rows
    R = nseg * L
    depth = norm_g_ref.shape[0]
    npool = POOL_HIST_ROWS - 1
    nconv = CONV_K - 1
    if not streamed:
        y_ref, pool_out_ref, conv_out_ref, pool_hist, conv_hist = refs[13:]
        vs_ref = None
        w_in_b, w_out_b = w_in_ref, w_out_ref
        tile = pl.program_id(1)
        mod_row = pl.program_id(0) % SUBLANES

        @pl.when(tile == 0)
        def _():
            pool_hist[...] = jnp.zeros_like(pool_hist)
            conv_hist[...] = jnp.zeros_like(conv_hist)

        def mod_rows(l, k):
            return mod_ref[l, k, pl.ds(mod_row, 1), :]
    else:
        pool_in_ref, conv_in_ref = refs[13:15]
        (y_ref, pool_out_ref, conv_out_ref, vs_ref, w_in_b, w_out_b,
         pool_hist, conv_hist, x_carry) = refs[15:]
        tile = 0
        step = pl.program_id(0)

        @pl.when(step == 0)
        def _():
            pool_hist[:, :, 0:1, :] = jnp.zeros((depth, nseg, 1, POOL_W), jnp.float32)
            pool_hist[:, :, 1:, :] = pool_in_ref[...]
            conv_hist[:, :, :CONV_HIST_ROWS - nconv, :] = jnp.zeros(
                (depth, nseg, CONV_HIST_ROWS - nconv, CONV_W), jnp.float32)
            conv_hist[:, :, CONV_HIST_ROWS - nconv:, :] = conv_in_ref[...]
            x_carry[...] = x_ref[...].reshape(R, D_MODEL)

        w_in_b[...] = w_in_ref[...].astype(jnp.bfloat16)
        w_out_b[...] = w_out_ref[...].astype(jnp.bfloat16)

        def mod_rows(l, k):
            m = mod_ref[l, k]
            return jnp.concatenate(
                [jnp.broadcast_to(m[i:i + 1], (L, D_MODEL)) for i in range(nseg)], axis=0)

    lane = lax.broadcasted_iota(jnp.int32, (1, LANES), 1)
    low_half = lane < POOL_GROUP

    inv_w = jnp.concatenate(
        [jnp.full((1, POOL_GROUP), 1.0 / w, jnp.float32) for w in POOL_WINDOWS], axis=-1)
    win = jnp.concatenate(
        [jnp.full((1, POOL_GROUP), float(w), jnp.float32) for w in POOL_WINDOWS], axis=-1)
    pos_top = (tile * L + pos0
               + lax.broadcasted_iota(jnp.int32, (POOL_HIST_ROWS, 1), 0)).astype(jnp.float32)
    cnt_top = jnp.minimum(pos_top + 1.0, win)

    def layer(l, wl, x):
        shift, scale, gate = mod_rows(l, 0), mod_rows(l, 1), mod_rows(l, 2)
        amp = _row(norm_g_ref, l) * (1.0 + scale)
        ms = jnp.mean(x * x, axis=-1, keepdims=True)
        hb = ((x * lax.rsqrt(ms + EPS)) * amp + shift).astype(jnp.bfloat16)
        z = jnp.dot(hb, w_in_b[wl], preferred_element_type=jnp.float32)
        p, zp, gb, gc, hx, zc, u, v, zg = (z[:, _OFFS[i]:_OFFS[i + 1]] for i in range(9))

        e = _ext_rows(pool_hist[l], p)
        s2 = e + pltpu.roll(e, 1, 0)
        s4 = s2 + pltpu.roll(s2, 2, 0)
        s4b = s4[:, LANES:]
        s8 = s4b + pltpu.roll(s4b, 4, 0)
        s16 = s8 + pltpu.roll(s8, 8, 0)
        wsum = jnp.concatenate([jnp.where(low_half, s2[:, :LANES], s4[:, :LANES]),
                                jnp.where(low_half, s8, s16)], axis=-1)
        wsum3 = _unext(wsum, nseg, POOL_HIST_ROWS).reshape(nseg, L, POOL_W)
        p3 = p.reshape(nseg, L, POOL_W)
        d_top = wsum3[:, :POOL_HIST_ROWS] / cnt_top[None] - p3[:, :POOL_HIST_ROWS]
        if L > POOL_HIST_ROWS:
            d_rest = wsum3[:, POOL_HIST_ROWS:] * inv_w[None] - p3[:, POOL_HIST_ROWS:]
            d3 = jnp.concatenate([d_top, d_rest], axis=1)
        else:
            d3 = d_top
        pooled = jnp.dot(d3.reshape(R, POOL_W).astype(jnp.bfloat16), wpool_ref[l],
                         preferred_element_type=jnp.float32) * _row(pscale_ref, l)
        y_pool = pooled * _silu(zp)
        pool_out_ref[l] = p3[:, L - npool:, :].reshape(pool_out_ref.shape[1:])
        if not streamed:
            pool_hist[l] = p3[:, L - POOL_HIST_ROWS:, :]

        q = gc * hx
        qe = _ext_rows(conv_hist[l], q)
        wc = wconv_ref[l]
        conv = wc[2:3] * qe + wc[1:2] * pltpu.roll(qe, 1, 0) + wc[0:1] * pltpu.roll(qe, 2, 0)
        y_conv = gb * _unext(conv, nseg, CONV_HIST_ROWS) * _silu(zc)
        q3 = q.reshape(nseg, L, CONV_W)
        conv_out_ref[l] = q3[:, L - nconv:, :].reshape(conv_out_ref.shape[1:])
        if not streamed:
            conv_hist[l] = q3[:, L - CONV_HIST_ROWS:, :]

        mu = jnp.mean(v, axis=-1, keepdims=True)
        vc = v - mu
        var = jnp.mean(vc * vc, axis=-1, keepdims=True)
        vn = (vc * lax.rsqrt(var + EPS)) * _row(vng_ref, l) + _row(vnb_ref, l)
        if vs_ref is not None:
            vs_ref[l] = vn.reshape(nseg, L, GMLP_W)
        vnb = vn.astype(jnp.bfloat16)
        zero = jnp.zeros((), jnp.bfloat16)
        mixed_chunks = []
        for c in range(R // gchunk):
            cols = []
            for j in range(GMLP_W // LANES):
                vcj = vnb[c * gchunk:(c + 1) * gchunk, j * LANES:(j + 1) * LANES]
                rhs = jnp.concatenate([jnp.where(low_half, vcj, zero), jnp.where(low_half, zero, vcj)],
                                      axis=0)
                cols.append(jnp.dot(glhs_ref[l, j], rhs, preferred_element_type=jnp.float32))
            mixed_chunks.append(jnp.concatenate(cols, axis=-1) + gbias_ref[l])
        mixed = mixed_chunks[0] if len(mixed_chunks) == 1 else jnp.concatenate(mixed_chunks, axis=0)
        y_gmlp = (u * mixed) * _silu(zg)

        ycat = jnp.concatenate([y_pool, y_conv, y_gmlp], axis=-1).astype(jnp.bfloat16)
        rb = min(OUT_ROW_BLOCK, R)
        y = jnp.concatenate(
            [jnp.dot(ycat[r:r + rb], w_out_b[wl], preferred_element_type=jnp.float32)
             for r in range(0, R, rb)], axis=0)
        return x + gate * y

    def final_norm(x):
        ms = jnp.mean(x * x, axis=-1, keepdims=True)
        y_ref[...] = ((x * lax.rsqrt(ms + EPS)) * fin_g_ref[...]).reshape(y_ref.shape)

    if not streamed:
        x = x_ref[...].reshape(R, D_MODEL)
        for l in range(depth):
            x = layer(l, l, x)
        final_norm(x)
    else:
        x = layer(step, 0, x_carry[...])
        x_carry[...] = x

        @pl.when(step == depth - 1)
        def _():
            final_norm(x)


def _const_spec(shape):
    nd = len(shape)
    return pl.BlockSpec(shape, lambda *_: (0,) * nd, pipeline_mode=pl.Buffered(1))


def _layer_spec(shape, buffers=2):
    nd = len(shape)
    return pl.BlockSpec((1,) + tuple(shape[1:]), lambda l: (l,) + (0,) * (nd - 1),
                        pipeline_mode=pl.Buffered(buffers))


def _gating_operands(w_s, b_s, n):
    blk = jnp.arange(n) // CHUNK
    mask = blk[None, :] <= blk[:, None]
    w = jnp.where(mask[None, None], w_s[:, :, :n, :n], 0.0).astype(jnp.bfloat16)
    lhs = jnp.concatenate([w[:, 0::2], w[:, 1::2]], axis=-1)
    bias = jnp.repeat(jnp.swapaxes(b_s[:, :, :n], 1, 2), GMLP_HEAD_DIM, axis=-1)
    return lhs, bias


def kernel(x_prompt, x_sample, state_pool, state_conv, c_prompt, c_sample, norm_g, w_ada, b_ada, w_in, w_pool, pool_scale, w_conv, v_norm_g, v_norm_b, w_s, b_s, w_out, final_norm_g):
    depth = w_in.shape[0]
    nb, seq, d = x_prompt.shape
    ndec, dec_seq, _ = x_sample.shape
    npool, nconv = state_pool.shape[2], state_conv.shape[2]
    assert npool == POOL_HIST_ROWS - 1 and nconv == CONV_K - 1 and d == D_MODEL
    f32, bf16 = jnp.float32, jnp.bfloat16

    mod_p, mod_s = pl.pallas_call(
        _mod_kernel,
        grid=(depth,),
        in_specs=[pl.BlockSpec((nb, d), lambda l: (0, 0)),
                  pl.BlockSpec((ndec, d), lambda l: (0, 0)),
                  pl.BlockSpec((1, d, 3 * d), lambda l: (l, 0, 0)),
                  pl.BlockSpec((depth, 3 * d), lambda l: (0, 0))],
        out_specs=[pl.BlockSpec((1, 3, nb, d), lambda l: (l, 0, 0, 0)),
                   pl.BlockSpec((1, 3, ndec, d), lambda l: (l, 0, 0, 0))],
        out_shape=[jax.ShapeDtypeStruct((depth, 3, nb, d), f32),
                   jax.ShapeDtypeStruct((depth, 3, ndec, d), f32)],
        name="adaln_mod",
    )(c_prompt, c_sample, w_ada, b_ada)

    ngroups = len(POOL_WINDOWS)
    wpool_bd = jnp.einsum('ab,lade->ladbe', jnp.eye(ngroups, dtype=f32), w_pool)
    wpool_bd = wpool_bd.reshape(depth, POOL_W, POOL_W).astype(bf16)
    fin_g2 = final_norm_g.reshape(1, d)
    small = (wpool_bd, pool_scale, w_conv, v_norm_g, v_norm_b)

    assert dec_seq >= POOL_HIST_ROWS and dec_seq % (2 * SUBLANES) == 0 and dec_seq <= CHUNK
    glhs_s, gbias_s = _gating_operands(w_s, b_s, dec_seq)
    sample_in = ((x_sample, mod_s, norm_g, w_in) + small + (glhs_s, gbias_s, w_out, fin_g2)
                 + (state_pool, state_conv))
    sample_specs = [_layer_spec(a.shape) if a is w_in or a is w_out
                    else _const_spec(a.shape) for a in sample_in]
    sample_out = [jax.ShapeDtypeStruct((ndec, dec_seq, d), f32),
                  jax.ShapeDtypeStruct((depth, ndec, npool, POOL_W), f32),
                  jax.ShapeDtypeStruct((depth, ndec, nconv, CONV_W), f32),
                  jax.ShapeDtypeStruct((depth, ndec, dec_seq, GMLP_W), f32),
                  jax.ShapeDtypeStruct(w_in.shape, bf16),
                  jax.ShapeDtypeStruct(w_out.shape, bf16)]
    sample_out_specs = [pl.BlockSpec(s.shape, lambda l, nd=len(s.shape): (0,) * nd)
                        for s in sample_out[:4]]
    sample_out_specs += [_layer_spec(w_in.shape, 1), _layer_spec(w_out.shape, 1)]
    y_sample, pool_s, conv_s, v_s, w_in_b, w_out_b = pl.pallas_call(
        functools.partial(_trunk_kernel, nseg=ndec, rows=dec_seq, gchunk=dec_seq,
                          streamed=True, pos0=PAST_LEN),
        grid=(depth,),
        in_specs=sample_specs,
        out_specs=sample_out_specs,
        out_shape=sample_out,
        scratch_shapes=[pltpu.VMEM((depth, ndec, POOL_HIST_ROWS, POOL_W), f32),
                        pltpu.VMEM((depth, ndec, CONV_HIST_ROWS, CONV_W), f32),
                        pltpu.VMEM((ndec * dec_seq, d), f32)],
        compiler_params=pltpu.CompilerParams(
            dimension_semantics=("arbitrary",), vmem_limit_bytes=VMEM_LIMIT_BYTES),
        name="sample_trunk",
    )(*sample_in)

    tl = min(SEQ_TILE, seq)
    assert seq % tl == 0 and tl % GMLP_CHUNK == 0 and nb % SUBLANES == 0
    glhs_p, gbias_p = _gating_operands(w_s, b_s, GMLP_CHUNK)
    prompt_in = (x_prompt, mod_p, norm_g, w_in_b) + small + (glhs_p, gbias_p, w_out_b, fin_g2)
    prompt_specs = [pl.BlockSpec((1, tl, d), lambda b, t: (b, t, 0)),
                    pl.BlockSpec((depth, 3, SUBLANES, d), lambda b, t: (0, 0, b // SUBLANES, 0))]
    prompt_specs += [_const_spec(a.shape) for a in prompt_in[2:]]
    y_prompt, pool_p, conv_p = pl.pallas_call(
        functools.partial(_trunk_kernel, nseg=1, rows=tl, gchunk=GMLP_CHUNK, streamed=False, pos0=0),
        grid=(nb, seq // tl),
        in_specs=prompt_specs,
        out_specs=[pl.BlockSpec((1, tl, d), lambda b, t: (b, t, 0)),
                   pl.BlockSpec((depth, 1, npool, POOL_W), lambda b, t: (0, b, 0, 0)),
                   pl.BlockSpec((depth, 1, nconv, CONV_W), lambda b, t: (0, b, 0, 0))],
        out_shape=[jax.ShapeDtypeStruct((nb, seq, d), f32),
                   jax.ShapeDtypeStruct((depth, nb, npool, POOL_W), f32),
                   jax.ShapeDtypeStruct((depth, nb, nconv, CONV_W), f32)],
        scratch_shapes=[pltpu.VMEM((depth, 1, POOL_HIST_ROWS, POOL_W), f32),
                        pltpu.VMEM((depth, 1, CONV_HIST_ROWS, CONV_W), f32)],
        compiler_params=pltpu.CompilerParams(
            dimension_semantics=("arbitrary", "arbitrary"), vmem_limit_bytes=VMEM_LIMIT_BYTES),
        name="prompt_trunk",
    )(*prompt_in)

    return y_prompt, y_sample, pool_p, conv_p, pool_s, conv_s, v_s
```

```python
import functools

import numpy as np
import jax
import jax.numpy as jnp
from jax import lax
from jax.experimental import pallas as pl
from jax.experimental.pallas import tpu as pltpu

D_MODEL = 1024
POOL_WINDOWS = (2, 4, 8, 16)
POOL_GROUP = 64
POOL_W = 256
CONV_W = 384
CONV_K = 3
GMLP_W = 384
GMLP_HEAD_DIM = 64
GMLP_CHUNK = 128
CHUNK = 64
PAST_LEN = 1024
EPS = 1e-6

_SPLITS = (256, 256, 384, 384, 384, 384, 384, 384, 384)
_OFFS = tuple(int(v) for v in np.cumsum((0,) + _SPLITS))

SUBLANES = 8
LANES = 128
POOL_HIST_ROWS = 2 * SUBLANES
CONV_HIST_ROWS = SUBLANES
CONV_HIST_LANES = 512
SEQ_TILE = 1024
OUT_ROW_BLOCK = 256
VMEM_LIMIT_BYTES = 56 * 1024 * 1024


def _silu(z):
    return z * (0.5 * jnp.tanh(0.5 * z) + 0.5)


def _mod_kernel(cp_ref, cs_ref, w_ref, b_ref, op_ref, os_ref):
    nb, d = cp_ref.shape
    layer = pl.program_id(0)
    c = jnp.concatenate([cp_ref[...], cs_ref[...]], axis=0)
    mod = jnp.dot(_silu(c), w_ref[0], preferred_element_type=jnp.float32) + b_ref[pl.ds(layer, 1), :]
    for k in range(3):
        op_ref[0, k] = mod[:nb, k * d:(k + 1) * d]
        os_ref[0, k] = mod[nb:, k * d:(k + 1) * d]


def _ext_rows(hist3, val2):
    nseg, h, c = hist3.shape
    l = val2.shape[0] // nseg
    return jnp.concatenate([hist3, val2.reshape(nseg, l, c)], axis=1).reshape(nseg * (h + l), c)


def _unext(e2, nseg, h):
    l = e2.shape[0] // nseg - h
    return e2.reshape(nseg, h + l, e2.shape[-1])[:, h:, :].reshape(nseg * l, e2.shape[-1])


def _row(ref, l):
    return ref[pl.ds(l, 1), :]


def _trunk_kernel(*refs, nseg, rows, gchunk, streamed, pos0):
    (x_ref, mod_ref, w_in_ref, wpool_ref, glhs_ref, gbias_ref, w_out_ref,
     norm_g_ref, pscale_ref, wconv_ref, vng_ref, vnb_ref, fin_g_ref) = refs[:13]
    L = rows
    R = nseg * L
    depth = norm_g_ref.shape[0]
    npool = POOL_HIST_ROWS - 1
    nconv = CONV_K - 1
    if not streamed:
        y_ref, pool_out_ref, conv_out_ref, pool_hist, conv_hist = refs[13:]
        vs_ref = None
        w_in_b, w_out_b = w_in_ref, w_out_ref
        tile = pl.program_id(1)
        mod_row = pl.program_id(0) % SUBLANES

        @pl.when(tile == 0)
        def _():
            pool_hist[...] = jnp.zeros_like(pool_hist)
            conv_hist[...] = jnp.zeros_like(conv_hist)

        def mod_rows(l, k):
            return mod_ref[l, k, pl.ds(mod_row, 1), :]
    else:
        pool_in_ref, conv_in_ref = refs[13:15]
        (y_ref, pool_out_ref, conv_out_ref, vs_ref, w_in_b, w_out_b,
         pool_hist, conv_hist, x_carry) = refs[15:]
        tile = 0
        step = pl.program_id(0)

        @pl.when(step == 0)
        def _():
            pool_hist[:, :, 0:1, :] = jnp.zeros((depth, nseg, 1, POOL_W), jnp.float32)
            pool_hist[:, :, 1:, :] = pool_in_ref[...]
            conv_hist[:, :, :CONV_HIST_ROWS - nconv, :] = jnp.zeros(
                (depth, nseg, CONV_HIST_ROWS - nconv, CONV_HIST_LANES), jnp.float32)
            conv_hist[:, :, CONV_HIST_ROWS - nconv:, :CONV_W] = conv_in_ref[...]
            x_carry[...] = x_ref[...].reshape(R, D_MODEL)

        w_in_b[...] = w_in_ref[...].astype(jnp.bfloat16)
        w_out_b[...] = w_out_ref[...].astype(jnp.bfloat16)

        def mod_rows(l, k):
            m = mod_ref[l, k]
            return jnp.concatenate(
                [jnp.broadcast_to(m[i:i + 1], (L, D_MODEL)) for i in range(nseg)], axis=0)

    lane = lax.broadcasted_iota(jnp.int32, (1, LANES), 1)
    low_half = lane < POOL_GROUP

    inv_w = jnp.concatenate(
        [jnp.full((1, POOL_GROUP), 1.0 / w, jnp.float32) for w in POOL_WINDOWS], axis=-1)
    win = jnp.concatenate(
        [jnp.full((1, POOL_GROUP), float(w), jnp.float32) for w in POOL_WINDOWS], axis=-1)
    pos_top = (tile * L + pos0
               + lax.broadcasted_iota(jnp.int32, (POOL_HIST_ROWS, 1), 0)).astype(jnp.float32)
    cnt_top = jnp.minimum(pos_top + 1.0, win)

    def layer(l, wl, x):
        shift, scale, gate = mod_rows(l, 0), mod_rows(l, 1), mod_rows(l, 2)
        amp = _row(norm_g_ref, l) * (1.0 + scale)
        ms = jnp.mean(x * x, axis=-1, keepdims=True)
        hb = ((x * lax.rsqrt(ms + EPS)) * amp + shift).astype(jnp.bfloat16)
        z = jnp.dot(hb, w_in_b[wl], preferred_element_type=jnp.float32)
        p, zp, gb, gc, hx, zc, u, v, zg = (z[:, _OFFS[i]:_OFFS[i + 1]] for i in range(9))

        e = _ext_rows(pool_hist[l], p)
        s2 = e + pltpu.roll(e, 1, 0)
        s4 = s2 + pltpu.roll(s2, 2, 0)
        s4b = s4[:, LANES:]
        s8 = s4b + pltpu.roll(s4b, 4, 0)
        s16 = s8 + pltpu.roll(s8, 8, 0)
        wsum = jnp.concatenate([jnp.where(low_half, s2[:, :LANES], s4[:, :LANES]),
                                jnp.where(low_half, s8, s16)], axis=-1)
        wsum3 = _unext(wsum, nseg, POOL_HIST_ROWS).reshape(nseg, L, POOL_W)
        p3 = p.reshape(nseg, L, POOL_W)
        d_top = wsum3[:, :POOL_HIST_ROWS] / cnt_top[None] - p3[:, :POOL_HIST_ROWS]
        if L > POOL_HIST_ROWS:
            d_rest = wsum3[:, POOL_HIST_ROWS:] * inv_w[None] - p3[:, POOL_HIST_ROWS:]
            d3 = jnp.concatenate([d_top, d_rest], axis=1)
        else:
            d3 = d_top
        pooled = jnp.dot(d3.reshape(R, POOL_W).astype(jnp.bfloat16), wpool_ref[l],
                         preferred_element_type=jnp.float32) * _row(pscale_ref, l)
        y_pool = pooled * _silu(zp)
        pool_out_ref[l] = p3[:, L - npool:, :].reshape(pool_out_ref.shape[1:])
        if not streamed:
            pool_hist[l] = p3[:, L - POOL_HIST_ROWS:, :]

        q = gc * hx
        qe = _ext_rows(conv_hist[l, :, :, :CONV_W], q)
        wc = wconv_ref[l]
        conv = wc[2:3] * qe + wc[1:2] * pltpu.roll(qe, 1, 0) + wc[0:1] * pltpu.roll(qe, 2, 0)
        y_conv = gb * _unext(conv, nseg, CONV_HIST_ROWS) * _silu(zc)
        q3 = q.reshape(nseg, L, CONV_W)
        if streamed:
            conv_out_ref[l] = q3[:, L - nconv:, :]
        else:
            conv_out_ref[l, pl.ds(mod_row, 1)] = q3[:, L - nconv:, :]
        if not streamed:
            conv_hist[l, :, :, :CONV_W] = q3[:, L - CONV_HIST_ROWS:, :]

        mu = jnp.mean(v, axis=-1, keepdims=True)
        vc = v - mu
        var = jnp.mean(vc * vc, axis=-1, keepdims=True)
        vn = (vc * lax.rsqrt(var + EPS)) * _row(vng_ref, l) + _row(vnb_ref, l)
        if vs_ref is not None:
            vs_ref[l] = vn.reshape(nseg, L, GMLP_W)
        vnb = vn.astype(jnp.bfloat16)
        zero = jnp.zeros((), jnp.bfloat16)
        mixed_chunks = []
        for c in range(R // gchunk):
            cols = []
            for j in range(GMLP_W // LANES):
                vcj = vnb[c * gchunk:(c + 1) * gchunk, j * LANES:(j + 1) * LANES]
                rhs = jnp.concatenate([jnp.where(low_half, vcj, zero), jnp.where(low_half, zero, vcj)],
                                      axis=0)
                cols.append(jnp.dot(glhs_ref[l, j], rhs, preferred_element_type=jnp.float32))
            mixed_chunks.append(jnp.concatenate(cols, axis=-1) + gbias_ref[l])
        mixed = mixed_chunks[0] if len(mixed_chunks) == 1 else jnp.concatenate(mixed_chunks, axis=0)
        y_gmlp = (u * mixed) * _silu(zg)

        ycat = jnp.concatenate([y_pool, y_conv, y_gmlp], axis=-1).astype(jnp.bfloat16)
        rb = min(OUT_ROW_BLOCK, R)
        y = jnp.concatenate(
            [jnp.dot(ycat[r:r + rb], w_out_b[wl], preferred_element_type=jnp.float32)
             for r in range(0, R, rb)], axis=0)
        return x + gate * y

    def final_norm(x):
        ms = jnp.mean(x * x, axis=-1, keepdims=True)
        y_ref[...] = ((x * lax.rsqrt(ms + EPS)) * fin_g_ref[...]).reshape(y_ref.shape)

    if not streamed:
        x = x_ref[...].reshape(R, D_MODEL)
        for l in range(depth):
            x = layer(l, l, x)
        final_norm(x)
    else:
        x = layer(step, 0, x_carry[...])
        x_carry[...] = x

        @pl.when(step == depth - 1)
        def _():
            final_norm(x)


def _const_spec(shape):
    nd = len(shape)
    return pl.BlockSpec(shape, lambda *_: (0,) * nd, pipeline_mode=pl.Buffered(1))


def _layer_spec(shape, buffers=2):
    nd = len(shape)
    return pl.BlockSpec((1,) + tuple(shape[1:]), lambda l: (l,) + (0,) * (nd - 1),
                        pipeline_mode=pl.Buffered(buffers))


def _gating_operands(w_s, b_s, n):
    blk = jnp.arange(n) // CHUNK
    mask = blk[None, :] <= blk[:, None]
    w = jnp.where(mask[None, None], w_s[:, :, :n, :n], 0.0).astype(jnp.bfloat16)
    lhs = jnp.concatenate([w[:, 0::2], w[:, 1::2]], axis=-1)
    bias = jnp.repeat(jnp.swapaxes(b_s[:, :, :n], 1, 2), GMLP_HEAD_DIM, axis=-1)
    return lhs, bias


def kernel(x_prompt, x_sample, state_pool, state_conv, c_prompt, c_sample, norm_g, w_ada, b_ada, w_in, w_pool, pool_scale, w_conv, v_norm_g, v_norm_b, w_s, b_s, w_out, final_norm_g):
    depth = w_in.shape[0]
    nb, seq, d = x_prompt.shape
    ndec, dec_seq, _ = x_sample.shape
    npool, nconv = state_pool.shape[2], state_conv.shape[2]
    assert npool == POOL_HIST_ROWS - 1 and nconv == CONV_K - 1 and d == D_MODEL
    f32, bf16 = jnp.float32, jnp.bfloat16

    mod_p, mod_s = pl.pallas_call(
        _mod_kernel,
        grid=(depth,),
        in_specs=[pl.BlockSpec((nb, d), lambda l: (0, 0)),
                  pl.BlockSpec((ndec, d), lambda l: (0, 0)),
                  pl.BlockSpec((1, d, 3 * d), lambda l: (l, 0, 0)),
                  pl.BlockSpec((depth, 3 * d), lambda l: (0, 0))],
        out_specs=[pl.BlockSpec((1, 3, nb, d), lambda l: (l, 0, 0, 0)),
                   pl.BlockSpec((1, 3, ndec, d), lambda l: (l, 0, 0, 0))],
        out_shape=[jax.ShapeDtypeStruct((depth, 3, nb, d), f32),
                   jax.ShapeDtypeStruct((depth, 3, ndec, d), f32)],
        name="adaln_mod",
    )(c_prompt, c_sample, w_ada, b_ada)

    ngroups = len(POOL_WINDOWS)
    wpool_bd = jnp.einsum('ab,lade->ladbe', jnp.eye(ngroups, dtype=f32), w_pool)
    wpool_bd = wpool_bd.reshape(depth, POOL_W, POOL_W).astype(bf16)
    fin_g2 = final_norm_g.reshape(1, d)
    small = (norm_g, pool_scale, w_conv, v_norm_g, v_norm_b, fin_g2)

    assert dec_seq >= POOL_HIST_ROWS and dec_seq % (2 * SUBLANES) == 0 and dec_seq <= CHUNK
    glhs_s, gbias_s = _gating_operands(w_s, b_s, dec_seq)
    sample_in = ((x_sample, mod_s, w_in, wpool_bd, glhs_s, gbias_s, w_out) + small
                 + (state_pool, state_conv))
    sample_specs = [_layer_spec(a.shape) if a is w_in or a is w_out
                    else _const_spec(a.shape) for a in sample_in]
    sample_out = [jax.ShapeDtypeStruct((ndec, dec_seq, d), f32),
                  jax.ShapeDtypeStruct((depth, ndec, npool, POOL_W), f32),
                  jax.ShapeDtypeStruct((depth, ndec, nconv, CONV_W), f32),
                  jax.ShapeDtypeStruct((depth, ndec, dec_seq, GMLP_W), f32),
                  jax.ShapeDtypeStruct(w_in.shape, bf16),
                  jax.ShapeDtypeStruct(w_out.shape, bf16)]
    sample_out_specs = [pl.BlockSpec(s.shape, lambda l, nd=len(s.shape): (0,) * nd)
                        for s in sample_out[:4]]
    sample_out_specs += [_layer_spec(w_in.shape, 1), _layer_spec(w_out.shape, 1)]
    y_sample, pool_s, conv_s, v_s, w_in_b, w_out_b = pl.pallas_call(
        functools.partial(_trunk_kernel, nseg=ndec, rows=dec_seq, gchunk=dec_seq,
                          streamed=True, pos0=PAST_LEN),
        grid=(depth,),
        in_specs=sample_specs,
        out_specs=sample_out_specs,
        out_shape=sample_out,
        scratch_shapes=[pltpu.VMEM((depth, ndec, POOL_HIST_ROWS, POOL_W), f32),
                        pltpu.VMEM((depth, ndec, CONV_HIST_ROWS, CONV_HIST_LANES), f32),
                        pltpu.VMEM((ndec * dec_seq, d), f32)],
        compiler_params=pltpu.CompilerParams(
            dimension_semantics=("arbitrary",), vmem_limit_bytes=VMEM_LIMIT_BYTES),
        name="sample_trunk",
    )(*sample_in)

    tl = min(SEQ_TILE, seq)
    assert seq % tl == 0 and tl % GMLP_CHUNK == 0 and nb % SUBLANES == 0
    glhs_p, gbias_p = _gating_operands(w_s, b_s, GMLP_CHUNK)
    prompt_in = (x_prompt, mod_p, w_in_b, wpool_bd, glhs_p, gbias_p, w_out_b) + small
    prompt_specs = [pl.BlockSpec((1, tl, d), lambda b, t: (b, t, 0)),
                    pl.BlockSpec((depth, 3, SUBLANES, d), lambda b, t: (0, 0, b // SUBLANES, 0))]
    prompt_specs += [_const_spec(a.shape) for a in prompt_in[2:]]
    y_prompt, pool_p, conv_p = pl.pallas_call(
        functools.partial(_trunk_kernel, nseg=1, rows=tl, gchunk=GMLP_CHUNK, streamed=False, pos0=0),
        grid=(nb, seq // tl),
        in_specs=prompt_specs,
        out_specs=[pl.BlockSpec((1, tl, d), lambda b, t: (b, t, 0)),
                   pl.BlockSpec((depth, 1, npool, POOL_W), lambda b, t: (0, b, 0, 0)),
                   pl.BlockSpec((depth, SUBLANES, nconv, CONV_W),
                                lambda b, t: (0, b // SUBLANES, 0, 0))],
        out_shape=[jax.ShapeDtypeStruct((nb, seq, d), f32),
                   jax.ShapeDtypeStruct((depth, nb, npool, POOL_W), f32),
                   jax.ShapeDtypeStruct((depth, nb, nconv, CONV_W), f32)],
        scratch_shapes=[pltpu.VMEM((depth, 1, POOL_HIST_ROWS, POOL_W), f32),
                        pltpu.VMEM((depth, 1, CONV_HIST_ROWS, CONV_HIST_LANES), f32)],
        compiler_params=pltpu.CompilerParams(
            dimension_semantics=("arbitrary", "arbitrary"), vmem_limit_bytes=VMEM_LIMIT_BYTES),
        name="prompt_trunk",
    )(*prompt_in)

    return y_prompt, y_sample, pool_p, conv_p, pool_s, conv_s, v_s
```

```python
import functools

import numpy as np
import jax
import jax.numpy as jnp
from jax import lax
from jax.experimental import pallas as pl
from jax.experimental.pallas import tpu as pltpu

D_MODEL = 1024
POOL_WINDOWS = (2, 4, 8, 16)
POOL_GROUP = 64
POOL_W = 256
CONV_W = 384
CONV_K = 3
GMLP_W = 384
GMLP_HEAD_DIM = 64
GMLP_CHUNK = 128
CHUNK = 64
PAST_LEN = 1024
EPS = 1e-6

_SPLITS = (256, 256, 384, 384, 384, 384, 384, 384, 384)
_OFFS = tuple(int(v) for v in np.cumsum((0,) + _SPLITS))

SUBLANES = 8
LANES = 128
POOL_HIST_ROWS = 2 * SUBLANES
CONV_HIST_ROWS = SUBLANES
SEQ_TILE = 1024
OUT_ROW_BLOCK = 256
VMEM_LIMIT_BYTES = 56 * 1024 * 1024


def _silu(z):
    h = 0.5 * z
    return h * jnp.tanh(h) + h


def _mod_kernel(cp_ref, cs_ref, w_ref, b_ref, op_ref, os_ref):
    nb, d = cp_ref.shape
    layer = pl.program_id(0)
    c = jnp.concatenate([cp_ref[...], cs_ref[...]], axis=0)
    mod = jnp.dot(_silu(c), w_ref[0], preferred_element_type=jnp.float32) + b_ref[pl.ds(layer, 1), :]
    for k in range(3):
        op_ref[0, k] = mod[:nb, k * d:(k + 1) * d]
        os_ref[0, k] = mod[nb:, k * d:(k + 1) * d]


def _ext_rows(hist3, val2):
    nseg, h, c = hist3.shape
    l = val2.shape[0] // nseg
    return jnp.concatenate([hist3, val2.reshape(nseg, l, c)], axis=1).reshape(nseg * (h + l), c)


def _unext(e2, nseg, h):
    l = e2.shape[0] // nseg - h
    return e2.reshape(nseg, h + l, e2.shape[-1])[:, h:, :].reshape(nseg * l, e2.shape[-1])


def _row(ref, l):
    return ref[pl.ds(l, 1), :]


def _trunk_kernel(*refs, nseg, rows, gchunk, streamed, pos0):
    (x_ref, mod_ref, norm_g_ref, w_in_ref, wpool_ref, pscale_ref, wconv_ref, vng_ref, vnb_ref,
     glhs_ref, gbias_ref, w_out_ref, fin_g_ref) = refs[:13]
    L = rows
    R = nseg * L
    depth = norm_g_ref.shape[0]
    npool = POOL_HIST_ROWS - 1
    nconv = CONV_K - 1
    if not streamed:
        y_ref, pool_out_ref, conv_out_ref, pool_hist, conv_hist = refs[13:]
        vs_ref = None
        w_in_b, w_out_b = w_in_ref, w_out_ref
        tile = pl.program_id(1)
        mod_row = pl.program_id(0) % SUBLANES

        @pl.when(tile == 0)
        def _():
            pool_hist[...] = jnp.zeros_like(pool_hist)
            conv_hist[...] = jnp.zeros_like(conv_hist)

        def mod_rows(l, k):
            return mod_ref[l, k, pl.ds(mod_row, 1), :]
    else:
        pool_in_ref, conv_in_ref = refs[13:15]
        (y_ref, pool_out_ref, conv_out_ref, vs_ref, w_in_b, w_out_b,
         pool_hist, conv_hist, x_carry) = refs[15:]
        tile = 0
        step = pl.program_id(0)

        @pl.when(step == 0)
        def _():
            pool_hist[:, :, 0:1, :] = jnp.zeros((depth, nseg, 1, POOL_W), jnp.float32)
            pool_hist[:, :, 1:, :] = pool_in_ref[...]
            conv_hist[:, :, :CONV_HIST_ROWS - nconv, :] = jnp.zeros(
                (depth, nseg, CONV_HIST_ROWS - nconv, CONV_W), jnp.float32)
            conv_hist[:, :, CONV_HIST_ROWS - nconv:, :] = conv_in_ref[...]
            x_carry[...] = x_ref[...].reshape(R, D_MODEL)

        w_in_b[...] = w_in_ref[...].astype(jnp.bfloat16)
        w_out_b[...] = w_out_ref[...].astype(jnp.bfloat16)

        def mod_rows(l, k):
            m = mod_ref[l, k]
            return jnp.concatenate(
                [jnp.broadcast_to(m[i:i + 1], (L, D_MODEL)) for i in range(nseg)], axis=0)

    lane = lax.broadcasted_iota(jnp.int32, (1, LANES), 1)
    low_half = lane < POOL_GROUP

    inv_w = jnp.concatenate(
        [jnp.full((1, POOL_GROUP), 1.0 / w, jnp.float32) for w in POOL_WINDOWS], axis=-1)
    win = jnp.concatenate(
        [jnp.full((1, POOL_GROUP), float(w), jnp.float32) for w in POOL_WINDOWS], axis=-1)
    pos_top = (tile * L + pos0
               + lax.broadcasted_iota(jnp.int32, (POOL_HIST_ROWS, 1), 0)).astype(jnp.float32)
    cnt_top = jnp.minimum(pos_top + 1.0, win)

    def layer(l, wl, x):
        shift, scale, gate = mod_rows(l, 0), mod_rows(l, 1), mod_rows(l, 2)
        amp = _row(norm_g_ref, l) * (1.0 + scale)
        ms = jnp.mean(x * x, axis=-1, keepdims=True)
        hb = ((x * lax.rsqrt(ms + EPS)) * amp + shift).astype(jnp.bfloat16)
        z = jnp.dot(hb, w_in_b[wl], preferred_element_type=jnp.float32)
        p, zp, gb, gc, hx, zc, u, v, zg = (z[:, _OFFS[i]:_OFFS[i + 1]] for i in range(9))

        e = _ext_rows(pool_hist[l], p)
        s2 = e + pltpu.roll(e, 1, 0)
        s4 = s2 + pltpu.roll(s2, 2, 0)
        s4b = s4[:, LANES:]
        s8 = s4b + pltpu.roll(s4b, 4, 0)
        s16 = s8 + pltpu.roll(s8, 8, 0)
        wsum = jnp.concatenate([jnp.where(low_half, s2[:, :LANES], s4[:, :LANES]),
                                jnp.where(low_half, s8, s16)], axis=-1)
        wsum3 = _unext(wsum, nseg, POOL_HIST_ROWS).reshape(nseg, L, POOL_W)
        p3 = p.reshape(nseg, L, POOL_W)
        d_top = wsum3[:, :POOL_HIST_ROWS] / cnt_top[None] - p3[:, :POOL_HIST_ROWS]
        if L > POOL_HIST_ROWS:
            d_rest = wsum3[:, POOL_HIST_ROWS:] * inv_w[None] - p3[:, POOL_HIST_ROWS:]
            d3 = jnp.concatenate([d_top, d_rest], axis=1)
        else:
            d3 = d_top
        pooled = jnp.dot(d3.reshape(R, POOL_W).astype(jnp.bfloat16), wpool_ref[l],
                         preferred_element_type=jnp.float32) * _row(pscale_ref, l)
        y_pool = pooled * _silu(zp)
        pool_out_ref[l] = p3[:, L - npool:, :].reshape(pool_out_ref.shape[1:])
        if not streamed:
            pool_hist[l] = p3[:, L - POOL_HIST_ROWS:, :]

        q = gc * hx
        qe = _ext_rows(conv_hist[l], q)
        wc = wconv_ref[l]
        conv = wc[2:3] * qe + wc[1:2] * pltpu.roll(qe, 1, 0) + wc[0:1] * pltpu.roll(qe, 2, 0)
        y_conv = gb * _unext(conv, nseg, CONV_HIST_ROWS) * _silu(zc)
        q3 = q.reshape(nseg, L, CONV_W)
        conv_out_ref[l] = q3[:, L - nconv:, :].reshape(conv_out_ref.shape[1:])
        if not streamed:
            conv_hist[l] = q3[:, L - CONV_HIST_ROWS:, :]

        mu = jnp.mean(v, axis=-1, keepdims=True)
        vc = v - mu
        var = jnp.mean(vc * vc, axis=-1, keepdims=True)
        vn = (vc * lax.rsqrt(var + EPS)) * _row(vng_ref, l) + _row(vnb_ref, l)
        if vs_ref is not None:
            vs_ref[l] = vn.reshape(nseg, L, GMLP_W)
        vnb = vn.astype(jnp.bfloat16)
        zero = jnp.zeros((), jnp.bfloat16)
        mixed_chunks = []
        for c in range(R // gchunk):
            cols = []
            for j in range(GMLP_W // LANES):
                vcj = vnb[c * gchunk:(c + 1) * gchunk, j * LANES:(j + 1) * LANES]
                rhs = jnp.concatenate([jnp.where(low_half, vcj, zero), jnp.where(low_half, zero, vcj)],
                                      axis=0)
                cols.append(jnp.dot(glhs_ref[l, j], rhs, preferred_element_type=jnp.float32))
            mixed_chunks.append(jnp.concatenate(cols, axis=-1) + gbias_ref[l])
        mixed = mixed_chunks[0] if len(mixed_chunks) == 1 else jnp.concatenate(mixed_chunks, axis=0)
        y_gmlp = (u * mixed) * _silu(zg)

        ycat = jnp.concatenate([y_pool, y_conv, y_gmlp], axis=-1).astype(jnp.bfloat16)
        rb = min(OUT_ROW_BLOCK, R)
        y = jnp.concatenate(
            [jnp.dot(ycat[r:r + rb], w_out_b[wl], preferred_element_type=jnp.float32)
             for r in range(0, R, rb)], axis=0)
        return x + gate * y

    def final_norm(x):
        ms = jnp.mean(x * x, axis=-1, keepdims=True)
        y_ref[...] = ((x * lax.rsqrt(ms + EPS)) * fin_g_ref[...]).reshape(y_ref.shape)

    if not streamed:
        x = x_ref[...].reshape(R, D_MODEL)
        for l in range(depth):
            x = layer(l, l, x)
        final_norm(x)
    else:
        x = layer(step, 0, x_carry[...])
        x_carry[...] = x

        @pl.when(step == depth - 1)
        def _():
            final_norm(x)


def _const_spec(shape):
    nd = len(shape)
    return pl.BlockSpec(shape, lambda *_: (0,) * nd, pipeline_mode=pl.Buffered(1))


def _layer_spec(shape, buffers=2):
    nd = len(shape)
    return pl.BlockSpec((1,) + tuple(shape[1:]), lambda l: (l,) + (0,) * (nd - 1),
                        pipeline_mode=pl.Buffered(buffers))


def _gating_operands(w_s, b_s, n):
    blk = jnp.arange(n) // CHUNK
    mask = blk[None, :] <= blk[:, None]
    w = jnp.where(mask[None, None], w_s[:, :, :n, :n], 0.0).astype(jnp.bfloat16)
    lhs = jnp.concatenate([w[:, 0::2], w[:, 1::2]], axis=-1)
    bias = jnp.repeat(jnp.swapaxes(b_s[:, :, :n], 1, 2), GMLP_HEAD_DIM, axis=-1)
    return lhs, bias


def kernel(x_prompt, x_sample, state_pool, state_conv, c_prompt, c_sample, norm_g, w_ada, b_ada, w_in, w_pool, pool_scale, w_conv, v_norm_g, v_norm_b, w_s, b_s, w_out, final_norm_g):
    depth = w_in.shape[0]
    nb, seq, d = x_prompt.shape
    ndec, dec_seq, _ = x_sample.shape
    npool, nconv = state_pool.shape[2], state_conv.shape[2]
    assert npool == POOL_HIST_ROWS - 1 and nconv == CONV_K - 1 and d == D_MODEL
    f32, bf16 = jnp.float32, jnp.bfloat16

    mod_p, mod_s = pl.pallas_call(
        _mod_kernel,
        grid=(depth,),
        in_specs=[pl.BlockSpec((nb, d), lambda l: (0, 0)),
                  pl.BlockSpec((ndec, d), lambda l: (0, 0)),
                  pl.BlockSpec((1, d, 3 * d), lambda l: (l, 0, 0)),
                  pl.BlockSpec((depth, 3 * d), lambda l: (0, 0))],
        out_specs=[pl.BlockSpec((1, 3, nb, d), lambda l: (l, 0, 0, 0)),
                   pl.BlockSpec((1, 3, ndec, d), lambda l: (l, 0, 0, 0))],
        out_shape=[jax.ShapeDtypeStruct((depth, 3, nb, d), f32),
                   jax.ShapeDtypeStruct((depth, 3, ndec, d), f32)],
        name="adaln_mod",
    )(c_prompt, c_sample, w_ada, b_ada)

    ngroups = len(POOL_WINDOWS)
    wpool_bd = jnp.einsum('ab,lade->ladbe', jnp.eye(ngroups, dtype=f32), w_pool)
    wpool_bd = wpool_bd.reshape(depth, POOL_W, POOL_W).astype(bf16)
    fin_g2 = final_norm_g.reshape(1, d)
    small = (wpool_bd, pool_scale, w_conv, v_norm_g, v_norm_b)

    assert dec_seq >= POOL_HIST_ROWS and dec_seq % (2 * SUBLANES) == 0 and dec_seq <= CHUNK
    glhs_s, gbias_s = _gating_operands(w_s, b_s, dec_seq)
    sample_in = ((x_sample, mod_s, norm_g, w_in) + small + (glhs_s, gbias_s, w_out, fin_g2)
                 + (state_pool, state_conv))
    sample_specs = [_layer_spec(a.shape) if a is w_in or a is w_out
                    else _const_spec(a.shape) for a in sample_in]
    sample_out = [jax.ShapeDtypeStruct((ndec, dec_seq, d), f32),
                  jax.ShapeDtypeStruct((depth, ndec, npool, POOL_W), f32),
                  jax.ShapeDtypeStruct((depth, ndec, nconv, CONV_W), f32),
                  jax.ShapeDtypeStruct((depth, ndec, dec_seq, GMLP_W), f32),
                  jax.ShapeDtypeStruct(w_in.shape, bf16),
                  jax.ShapeDtypeStruct(w_out.shape, bf16)]
    sample_out_specs = [pl.BlockSpec(s.shape, lambda l, nd=len(s.shape): (0,) * nd)
                        for s in sample_out[:4]]
    sample_out_specs += [_layer_spec(w_in.shape, 1), _layer_spec(w_out.shape, 1)]
    y_sample, pool_s, conv_s, v_s, w_in_b, w_out_b = pl.pallas_call(
        functools.partial(_trunk_kernel, nseg=ndec, rows=dec_seq, gchunk=dec_seq,
                          streamed=True, pos0=PAST_LEN),
        grid=(depth,),
        in_specs=sample_specs,
        out_specs=sample_out_specs,
        out_shape=sample_out,
        scratch_shapes=[pltpu.VMEM((depth, ndec, POOL_HIST_ROWS, POOL_W), f32),
                        pltpu.VMEM((depth, ndec, CONV_HIST_ROWS, CONV_W), f32),
                        pltpu.VMEM((ndec * dec_seq, d), f32)],
        compiler_params=pltpu.CompilerParams(
            dimension_semantics=("arbitrary",), vmem_limit_bytes=VMEM_LIMIT_BYTES),
        name="sample_trunk",
    )(*sample_in)

    tl = min(SEQ_TILE, seq)
    assert seq % tl == 0 and tl % GMLP_CHUNK == 0 and nb % SUBLANES == 0
    glhs_p, gbias_p = _gating_operands(w_s, b_s, GMLP_CHUNK)
    prompt_in = (x_prompt, mod_p, norm_g, w_in_b) + small + (glhs_p, gbias_p, w_out_b, fin_g2)
    prompt_specs = [pl.BlockSpec((1, tl, d), lambda b, t: (b, t, 0)),
                    pl.BlockSpec((depth, 3, SUBLANES, d), lambda b, t: (0, 0, b // SUBLANES, 0))]
    prompt_specs += [_const_spec(a.shape) for a in prompt_in[2:]]
    y_prompt, pool_p, conv_p = pl.pallas_call(
        functools.partial(_trunk_kernel, nseg=1, rows=tl, gchunk=GMLP_CHUNK, streamed=False, pos0=0),
        grid=(nb, seq // tl),
        in_specs=prompt_specs,
        out_specs=[pl.BlockSpec((1, tl, d), lambda b, t: (b, t, 0)),
                   pl.BlockSpec((depth, 1, npool, POOL_W), lambda b, t: (0, b, 0, 0)),
                   pl.BlockSpec((depth, 1, nconv, CONV_W), lambda b, t: (0, b, 0, 0))],
        out_shape=[jax.ShapeDtypeStruct((nb, seq, d), f32),
                   jax.ShapeDtypeStruct((depth, nb, npool, POOL_W), f32),
                   jax.ShapeDtypeStruct((depth, nb, nconv, CONV_W), f32)],
        scratch_shapes=[pltpu.VMEM((depth, 1, POOL_HIST_ROWS, POOL_W), f32),
                        pltpu.VMEM((depth, 1, CONV_HIST_ROWS, CONV_W), f32)],
        compiler_params=pltpu.CompilerParams(
            dimension_semantics=("arbitrary", "arbitrary"), vmem_limit_bytes=VMEM_LIMIT_BYTES),
        name="prompt_trunk",
    )(*prompt_in)

    return y_prompt, y_sample, pool_p, conv_p, pool_s, conv_s, v_s
```

```python
import functools

import numpy as np
import jax
import jax.numpy as jnp
from jax import lax
from jax.experimental import pallas as pl
from jax.experimental.pallas import tpu as pltpu

D_MODEL = 1024
POOL_WINDOWS = (2, 4, 8, 16)
POOL_GROUP = 64
POOL_W = 256
CONV_W = 384
CONV_K = 3
GMLP_W = 384
GMLP_HEAD_DIM = 64
GMLP_CHUNK = 128
CHUNK = 64
PAST_LEN = 1024
EPS = 1e-6

_SPLITS = (256, 256, 384, 384, 384, 384, 384, 384, 384)
_OFFS = tuple(int(v) for v in np.cumsum((0,) + _SPLITS))

SUBLANES = 8
LANES = 128
POOL_HIST_ROWS = 2 * SUBLANES
CONV_HIST_ROWS = SUBLANES
SEQ_TILE = 1024
OUT_ROW_BLOCK = 256
GMLP_CHUNKS_PER_DOT = 2
VMEM_LIMIT_BYTES = 56 * 1024 * 1024


def _silu(z):
    return z * (0.5 * jnp.tanh(0.5 * z) + 0.5)


def _mod_kernel(cp_ref, cs_ref, w_ref, b_ref, op_ref, os_ref):
    nb, d = cp_ref.shape
    layer = pl.program_id(0)
    c = jnp.concatenate([cp_ref[...], cs_ref[...]], axis=0)
    mod = jnp.dot(_silu(c), w_ref[0], preferred_element_type=jnp.float32) + b_ref[pl.ds(layer, 1), :]
    for k in range(3):
        op_ref[0, k] = mod[:nb, k * d:(k + 1) * d]
        os_ref[0, k] = mod[nb:, k * d:(k + 1) * d]


def _ext_rows(hist3, val2):
    nseg, h, c = hist3.shape
    l = val2.shape[0] // nseg
    return jnp.concatenate([hist3, val2.reshape(nseg, l, c)], axis=1).reshape(nseg * (h + l), c)


def _unext(e2, nseg, h):
    l = e2.shape[0] // nseg - h
    return e2.reshape(nseg, h + l, e2.shape[-1])[:, h:, :].reshape(nseg * l, e2.shape[-1])


def _row(ref, l):
    return ref[pl.ds(l, 1), :]


def _trunk_kernel(*refs, nseg, rows, gchunk, streamed, pos0):
    (x_ref, mod_ref, norm_g_ref, w_in_ref, wpool_ref, pscale_ref, wconv_ref, vng_ref, vnb_ref,
     glhs_ref, gbias_ref, w_out_ref, fin_g_ref) = refs[:13]
    L = rows
    R = nseg * L
    depth = norm_g_ref.shape[0]
    npool = POOL_HIST_ROWS - 1
    nconv = CONV_K - 1
    if not streamed:
        y_ref, pool_out_ref, conv_out_ref, pool_hist, conv_hist = refs[13:]
        vs_ref = None
        w_in_b, w_out_b = w_in_ref, w_out_ref
        tile = pl.program_id(1)
        mod_row = pl.program_id(0) % SUBLANES

        @pl.when(tile == 0)
        def _():
            pool_hist[...] = jnp.zeros_like(pool_hist)
            conv_hist[...] = jnp.zeros_like(conv_hist)

        def mod_rows(l, k):
            return mod_ref[l, k, pl.ds(mod_row, 1), :]
    else:
        pool_in_ref, conv_in_ref = refs[13:15]
        (y_ref, pool_out_ref, conv_out_ref, vs_ref, w_in_b, w_out_b,
         pool_hist, conv_hist, x_carry) = refs[15:]
        tile = 0
        step = pl.program_id(0)

        @pl.when(step == 0)
        def _():
            pool_hist[:, :, 0:1, :] = jnp.zeros((depth, nseg, 1, POOL_W), jnp.float32)
            pool_hist[:, :, 1:, :] = pool_in_ref[...]
            conv_hist[:, :, :CONV_HIST_ROWS - nconv, :] = jnp.zeros(
                (depth, nseg, CONV_HIST_ROWS - nconv, CONV_W), jnp.float32)
            conv_hist[:, :, CONV_HIST_ROWS - nconv:, :] = conv_in_ref[...]
            x_carry[...] = x_ref[...].reshape(R, D_MODEL)

        w_in_b[...] = w_in_ref[...].astype(jnp.bfloat16)
        w_out_b[...] = w_out_ref[...].astype(jnp.bfloat16)

        def mod_rows(l, k):
            m = mod_ref[l, k]
            return jnp.concatenate(
                [jnp.broadcast_to(m[i:i + 1], (L, D_MODEL)) for i in range(nseg)], axis=0)

    lane = lax.broadcasted_iota(jnp.int32, (1, LANES), 1)
    low_half = lane < POOL_GROUP

    inv_w = jnp.concatenate(
        [jnp.full((1, POOL_GROUP), 1.0 / w, jnp.float32) for w in POOL_WINDOWS], axis=-1)
    win = jnp.concatenate(
        [jnp.full((1, POOL_GROUP), float(w), jnp.float32) for w in POOL_WINDOWS], axis=-1)
    pos_top = (tile * L + pos0
               + lax.broadcasted_iota(jnp.int32, (POOL_HIST_ROWS, 1), 0)).astype(jnp.float32)
    cnt_top = jnp.minimum(pos_top + 1.0, win)

    def layer(l, wl, x):
        shift, scale, gate = mod_rows(l, 0), mod_rows(l, 1), mod_rows(l, 2)
        amp = _row(norm_g_ref, l) * (1.0 + scale)
        ms = jnp.mean(x * x, axis=-1, keepdims=True)
        hb = ((x * lax.rsqrt(ms + EPS)) * amp + shift).astype(jnp.bfloat16)
        z = jnp.dot(hb, w_in_b[wl], preferred_element_type=jnp.float32)
        p, zp, gb, gc, hx, zc, u, v, zg = (z[:, _OFFS[i]:_OFFS[i + 1]] for i in range(9))

        e = _ext_rows(pool_hist[l], p)
        s2 = e + pltpu.roll(e, 1, 0)
        s4 = s2 + pltpu.roll(s2, 2, 0)
        s4b = s4[:, LANES:]
        s8 = s4b + pltpu.roll(s4b, 4, 0)
        s16 = s8 + pltpu.roll(s8, 8, 0)
        wsum = jnp.concatenate([jnp.where(low_half, s2[:, :LANES], s4[:, :LANES]),
                                jnp.where(low_half, s8, s16)], axis=-1)
        wsum3 = _unext(wsum, nseg, POOL_HIST_ROWS).reshape(nseg, L, POOL_W)
        p3 = p.reshape(nseg, L, POOL_W)
        d_top = wsum3[:, :POOL_HIST_ROWS] / cnt_top[None] - p3[:, :POOL_HIST_ROWS]
        if L > POOL_HIST_ROWS:
            d_rest = wsum3[:, POOL_HIST_ROWS:] * inv_w[None] - p3[:, POOL_HIST_ROWS:]
            d3 = jnp.concatenate([d_top, d_rest], axis=1)
        else:
            d3 = d_top
        pooled = jnp.dot(d3.reshape(R, POOL_W).astype(jnp.bfloat16), wpool_ref[l],
                         preferred_element_type=jnp.float32) * _row(pscale_ref, l)
        y_pool = pooled * _silu(zp)
        pool_out_ref[l] = p3[:, L - npool:, :].reshape(pool_out_ref.shape[1:])
        if not streamed:
            pool_hist[l] = p3[:, L - POOL_HIST_ROWS:, :]

        q = gc * hx
        qe = _ext_rows(conv_hist[l], q)
        wc = wconv_ref[l]
        conv = wc[2:3] * qe + wc[1:2] * pltpu.roll(qe, 1, 0) + wc[0:1] * pltpu.roll(qe, 2, 0)
        y_conv = gb * _unext(conv, nseg, CONV_HIST_ROWS) * _silu(zc)
        q3 = q.reshape(nseg, L, CONV_W)
        conv_out_ref[l] = q3[:, L - nconv:, :].reshape(conv_out_ref.shape[1:])
        if not streamed:
            conv_hist[l] = q3[:, L - CONV_HIST_ROWS:, :]

        mu = jnp.mean(v, axis=-1, keepdims=True)
        vc = v - mu
        var = jnp.mean(vc * vc, axis=-1, keepdims=True)
        vn = (vc * lax.rsqrt(var + EPS)) * _row(vng_ref, l) + _row(vnb_ref, l)
        if vs_ref is not None:
            vs_ref[l] = vn.reshape(nseg, L, GMLP_W)
        vnb = vn.astype(jnp.bfloat16)
        zero = jnp.zeros((), jnp.bfloat16)
        nchunk = R // gchunk
        cpd = GMLP_CHUNKS_PER_DOT if nchunk % GMLP_CHUNKS_PER_DOT == 0 else 1
        mixed_chunks = [[] for _ in range(nchunk)]
        for c in range(0, nchunk, cpd):
            for j in range(GMLP_W // LANES):
                rhs = []
                for cc in range(c, c + cpd):
                    vcj = vnb[cc * gchunk:(cc + 1) * gchunk, j * LANES:(j + 1) * LANES]
                    rhs.append(jnp.concatenate(
                        [jnp.where(low_half, vcj, zero), jnp.where(low_half, zero, vcj)], axis=0))
                rhs = rhs[0] if cpd == 1 else jnp.concatenate(rhs, axis=-1)
                m = jnp.dot(glhs_ref[l, j], rhs, preferred_element_type=jnp.float32)
                for k in range(cpd):
                    mixed_chunks[c + k].append(m[:, k * LANES:(k + 1) * LANES])
        mixed = jnp.concatenate(
            [jnp.concatenate(cols, axis=-1) + gbias_ref[l] for cols in mixed_chunks], axis=0)
        y_gmlp = (u * mixed) * _silu(zg)

        ycat = jnp.concatenate([y_pool, y_conv, y_gmlp], axis=-1).astype(jnp.bfloat16)
        rb = min(OUT_ROW_BLOCK, R)
        y = jnp.concatenate(
            [jnp.dot(ycat[r:r + rb], w_out_b[wl], preferred_element_type=jnp.float32)
             for r in range(0, R, rb)], axis=0)
        return x + gate * y

    def final_norm(x):
        ms = jnp.mean(x * x, axis=-1, keepdims=True)
        y_ref[...] = ((x * lax.rsqrt(ms + EPS)) * fin_g_ref[...]).reshape(y_ref.shape)

    if not streamed:
        x = x_ref[...].reshape(R, D_MODEL)
        for l in range(depth):
            x = layer(l, l, x)
        final_norm(x)
    else:
        x = layer(step, 0, x_carry[...])
        x_carry[...] = x

        @pl.when(step == depth - 1)
        def _():
            final_norm(x)


def _const_spec(shape):
    nd = len(shape)
    return pl.BlockSpec(shape, lambda *_: (0,) * nd, pipeline_mode=pl.Buffered(1))


def _layer_spec(shape, buffers=2):
    nd = len(shape)
    return pl.BlockSpec((1,) + tuple(shape[1:]), lambda l: (l,) + (0,) * (nd - 1),
                        pipeline_mode=pl.Buffered(buffers))


def _gating_operands(w_s, b_s, n):
    blk = jnp.arange(n) // CHUNK
    mask = blk[None, :] <= blk[:, None]
    w = jnp.where(mask[None, None], w_s[:, :, :n, :n], 0.0).astype(jnp.bfloat16)
    lhs = jnp.concatenate([w[:, 0::2], w[:, 1::2]], axis=-1)
    bias = jnp.repeat(jnp.swapaxes(b_s[:, :, :n], 1, 2), GMLP_HEAD_DIM, axis=-1)
    return lhs, bias


def kernel(x_prompt, x_sample, state_pool, state_conv, c_prompt, c_sample, norm_g, w_ada, b_ada, w_in, w_pool, pool_scale, w_conv, v_norm_g, v_norm_b, w_s, b_s, w_out, final_norm_g):
    depth = w_in.shape[0]
    nb, seq, d = x_prompt.shape
    ndec, dec_seq, _ = x_sample.shape
    npool, nconv = state_pool.shape[2], state_conv.shape[2]
    assert npool == POOL_HIST_ROWS - 1 and nconv == CONV_K - 1 and d == D_MODEL
    f32, bf16 = jnp.float32, jnp.bfloat16

    mod_p, mod_s = pl.pallas_call(
        _mod_kernel,
        grid=(depth,),
        in_specs=[pl.BlockSpec((nb, d), lambda l: (0, 0)),
                  pl.BlockSpec((ndec, d), lambda l: (0, 0)),
                  pl.BlockSpec((1, d, 3 * d), lambda l: (l, 0, 0)),
                  pl.BlockSpec((depth, 3 * d), lambda l: (0, 0))],
        out_specs=[pl.BlockSpec((1, 3, nb, d), lambda l: (l, 0, 0, 0)),
                   pl.BlockSpec((1, 3, ndec, d), lambda l: (l, 0, 0, 0))],
        out_shape=[jax.ShapeDtypeStruct((depth, 3, nb, d), f32),
                   jax.ShapeDtypeStruct((depth, 3, ndec, d), f32)],
        name="adaln_mod",
    )(c_prompt, c_sample, w_ada, b_ada)

    ngroups = len(POOL_WINDOWS)
    wpool_bd = jnp.einsum('ab,lade->ladbe', jnp.eye(ngroups, dtype=f32), w_pool)
    wpool_bd = wpool_bd.reshape(depth, POOL_W, POOL_W).astype(bf16)
    fin_g2 = final_norm_g.reshape(1, d)
    small = (wpool_bd, pool_scale, w_conv, v_norm_g, v_norm_b)

    assert dec_seq >= POOL_HIST_ROWS and dec_seq % (2 * SUBLANES) == 0 and dec_seq <= CHUNK
    glhs_s, gbias_s = _gating_operands(w_s, b_s, dec_seq)
    sample_in = ((x_sample, mod_s, norm_g, w_in) + small + (glhs_s, gbias_s, w_out, fin_g2)
                 + (state_pool, state_conv))
    sample_specs = [_layer_spec(a.shape) if a is w_in or a is w_out
                    else _const_spec(a.shape) for a in sample_in]
    sample_out = [jax.ShapeDtypeStruct((ndec, dec_seq, d), f32),
                  jax.ShapeDtypeStruct((depth, ndec, npool, POOL_W), f32),
                  jax.ShapeDtypeStruct((depth, ndec, nconv, CONV_W), f32),
                  jax.ShapeDtypeStruct((depth, ndec, dec_seq, GMLP_W), f32),
                  jax.ShapeDtypeStruct(w_in.shape, bf16),
                  jax.ShapeDtypeStruct(w_out.shape, bf16)]
    sample_out_specs = [pl.BlockSpec(s.shape, lambda l, nd=len(s.shape): (0,) * nd)
                        for s in sample_out[:4]]
    sample_out_specs += [_layer_spec(w_in.shape, 1), _layer_spec(w_out.shape, 1)]
    y_sample, pool_s, conv_s, v_s, w_in_b, w_out_b = pl.pallas_call(
        functools.partial(_trunk_kernel, nseg=ndec, rows=dec_seq, gchunk=dec_seq,
                          streamed=True, pos0=PAST_LEN),
        grid=(depth,),
        in_specs=sample_specs,
        out_specs=sample_out_specs,
        out_shape=sample_out,
        scratch_shapes=[pltpu.VMEM((depth, ndec, POOL_HIST_ROWS, POOL_W), f32),
                        pltpu.VMEM((depth, ndec, CONV_HIST_ROWS, CONV_W), f32),
                        pltpu.VMEM((ndec * dec_seq, d), f32)],
        compiler_params=pltpu.CompilerParams(
            dimension_semantics=("arbitrary",), vmem_limit_bytes=VMEM_LIMIT_BYTES),
        name="sample_trunk",
    )(*sample_in)

    tl = min(SEQ_TILE, seq)
    assert seq % tl == 0 and tl % GMLP_CHUNK == 0 and nb % SUBLANES == 0
    glhs_p, gbias_p = _gating_operands(w_s, b_s, GMLP_CHUNK)
    prompt_in = (x_prompt, mod_p, norm_g, w_in_b) + small + (glhs_p, gbias_p, w_out_b, fin_g2)
    prompt_specs = [pl.BlockSpec((1, tl, d), lambda b, t: (b, t, 0)),
                    pl.BlockSpec((depth, 3, SUBLANES, d), lambda b, t: (0, 0, b // SUBLANES, 0))]
    prompt_specs += [_const_spec(a.shape) for a in prompt_in[2:]]
    y_prompt, pool_p, conv_p = pl.pallas_call(
        functools.partial(_trunk_kernel, nseg=1, rows=tl, gchunk=GMLP_CHUNK, streamed=False, pos0=0),
        grid=(nb, seq // tl),
        in_specs=prompt_specs,
        out_specs=[pl.BlockSpec((1, tl, d), lambda b, t: (b, t, 0)),
                   pl.BlockSpec((depth, 1, npool, POOL_W), lambda b, t: (0, b, 0, 0)),
                   pl.BlockSpec((depth, 1, nconv, CONV_W), lambda b, t: (0, b, 0, 0))],
        out_shape=[jax.ShapeDtypeStruct((nb, seq, d), f32),
                   jax.ShapeDtypeStruct((depth, nb, npool, POOL_W), f32),
                   jax.ShapeDtypeStruct((depth, nb, nconv, CONV_W), f32)],
        scratch_shapes=[pltpu.VMEM((depth, 1, POOL_HIST_ROWS, POOL_W), f32),
                        pltpu.VMEM((depth, 1, CONV_HIST_ROWS, CONV_W), f32)],
        compiler_params=pltpu.CompilerParams(
            dimension_semantics=("arbitrary", "arbitrary"), vmem_limit_bytes=VMEM_LIMIT_BYTES),
        name="prompt_trunk",
    )(*prompt_in)

    return y_prompt, y_sample, pool_p, conv_p, pool_s, conv_s, v_s
```
